```python
import functools
import jax, jax.numpy as jnp
from jax import lax
import numpy as np


D_MODEL = 1024
BATCH = 1
SEQ = 16384
DEPTH = 2
DEC_BATCH = 32
DEC_SEQ = 4
PAST_LEN = 16384
PAGE_SIZE = 128

HEAD_DIM = 64
N_HEADS_A = 16
N_KV_A = 4
GROUP_A = N_HEADS_A // N_KV_A
N_HEADS_B = 16
CMP_BLOCK = 32
CMP_STRIDE = 16
CMP_HIDDEN = 256
SEL_BLOCK = 64
TOP_N = 16
WINDOW = 512
Q_BLOCK = 128
D_FF = 2816
N_EXPERTS = 8
TOP_K = 2
N_LAYERS_A = (DEPTH + 1) // 2
N_LAYERS_B = DEPTH // 2
RMS_EPS = 1e-6
FORCE_SCORE = 1e4
ATTN_SCALE = HEAD_DIM ** -0.5
FORGET_BIAS_INIT = 3.0
A_Q_COLS = N_HEADS_A * HEAD_DIM
A_KV_COLS = 2 * N_KV_A * HEAD_DIM
A_IN_COLS = A_Q_COLS + 3 * A_KV_COLS + 3 * N_HEADS_A
B_HEAD_COLS = N_HEADS_B * HEAD_DIM
B_IN_COLS = 3 * B_HEAD_COLS + N_HEADS_B

kernel_name = 'nsa_fox_hybrid_decode_step'


def rms_norm(x, g):
    xf = x.astype(jnp.float32)
    y = xf * lax.rsqrt(jnp.mean(xf * xf, axis=-1, keepdims=True) + RMS_EPS)
    return (y * g.astype(jnp.float32)).astype(x.dtype)


def masked_softmax(s, mask):
    s = jnp.where(mask, s.astype(jnp.float32), -jnp.inf)
    m = jnp.max(s, axis=-1, keepdims=True)
    m = jnp.where(jnp.isfinite(m), m, 0.0)
    p = jnp.exp(s - m)
    return p / jnp.maximum(jnp.sum(p, axis=-1, keepdims=True), 1e-30)


def alibi_slopes(n_heads):
    return (2.0 ** (-8.0 * np.arange(1, n_heads + 1) / n_heads)).astype(np.float32)


def overlap_matrix(n_cmp, n_sel):
    cs = np.arange(n_cmp)[:, None] * CMP_STRIDE
    ss = np.arange(n_sel)[None, :] * SEL_BLOCK
    ov = np.minimum(cs + CMP_BLOCK, ss + SEL_BLOCK) - np.maximum(cs, ss)
    return (np.clip(ov, 0, None) / CMP_BLOCK).astype(np.float32)


def swiglu(h, wg, wu, wd):
    return (jax.nn.silu(h @ wg) * (h @ wu)) @ wd


def moe_ffn(h, router, wg, wu, wd):
    shp = h.shape
    t = h.reshape(-1, shp[-1])
    probs = jax.nn.softmax((t @ router).astype(jnp.float32), axis=-1)
    top_p, top_i = lax.top_k(probs, TOP_K)
    top_p = top_p / jnp.sum(top_p, axis=-1, keepdims=True)
    gate = jnp.einsum('nk,nke->ne', top_p, jax.nn.one_hot(top_i, N_EXPERTS, dtype=jnp.float32))
    out = jnp.zeros(t.shape, jnp.float32)
    for e in range(N_EXPERTS):
        out = out + gate[:, e:e + 1] * swiglu(t, wg[e], wu[e], wd[e])
    return out.reshape(shp)


def nsa_project(h, params):
    w_in, q_norm, k_norm, gate_b = params[:4]
    T = h.shape[0]
    proj = h @ w_in
    q = rms_norm(proj[:, :A_Q_COLS].reshape(T, N_HEADS_A, HEAD_DIM), q_norm)
    kv = proj[:, A_Q_COLS:A_Q_COLS + 3 * A_KV_COLS].reshape(T, 3, 2, N_KV_A, HEAD_DIM)
    kv_cmp = kv[:, 0]
    kv_sel = jnp.stack([rms_norm(kv[:, 1, 0], k_norm[1]), kv[:, 1, 1]], axis=1)
    kv_win = jnp.stack([rms_norm(kv[:, 2, 0], k_norm[2]), kv[:, 2, 1]], axis=1)
    gates = jax.nn.sigmoid((proj[:, A_Q_COLS + 3 * A_KV_COLS:] + gate_b).astype(jnp.float32))
    return q, kv_cmp, kv_sel, kv_win, gates.reshape(T, 3, N_HEADS_A)


def compress_rows(rows, pos_emb, w1, w2):
    L = rows.shape[0]
    n_cmp = (L - CMP_BLOCK) // CMP_STRIDE + 1
    idx = np.arange(n_cmp)[:, None] * CMP_STRIDE + np.arange(CMP_BLOCK)[None, :]
    blk = rows[idx] + pos_emb[None, :, None, :]
    blk = jnp.transpose(blk, (0, 2, 1, 3)).reshape(n_cmp, N_KV_A, CMP_BLOCK * HEAD_DIM)
    return jax.nn.silu(blk @ w1) @ w2


def nsa_global_keys(kv_cmp, kv_sel, params):
    k_norm, cmp_pos, cmp_w1, cmp_w2 = params[2], params[4], params[5], params[6]
    L = kv_cmp.shape[0]
    kc = rms_norm(compress_rows(kv_cmp[:, 0], cmp_pos[0], cmp_w1[0], cmp_w2[0]), k_norm[0])
    vc = compress_rows(kv_cmp[:, 1], cmp_pos[1], cmp_w1[1], cmp_w2[1])
    n_sel = -(-L // SEL_BLOCK)
    sel = jnp.pad(kv_sel, ((0, n_sel * SEL_BLOCK - L), (0, 0), (0, 0), (0, 0)))
    sel = sel.reshape(n_sel, SEL_BLOCK, 2, N_KV_A, HEAD_DIM).transpose(2, 3, 0, 1, 4)
    return kc, vc, sel


def nsa_attend(q, qpos, gates, kc, vc, sel_blocks, kv_win, kwpos):
    Tq = q.shape[0]
    n_cmp = kc.shape[0]
    n_sel = sel_blocks.shape[2]
    slopes = alibi_slopes(N_HEADS_A).reshape(N_KV_A, GROUP_A)
    qg = q.reshape(Tq, N_KV_A, GROUP_A, HEAD_DIM)
    cend = np.arange(n_cmp) * CMP_STRIDE + (CMP_BLOCK - 1)
    d_cmp = qpos[:, None] - cend[None, :]
    s = jnp.einsum('tgrd,ngd->tgrn', qg, kc) * ATTN_SCALE
    s = s - slopes[None, :, :, None] * d_cmp[:, None, None, :].astype(jnp.float32)
    p_cmp = masked_softmax(s, (d_cmp >= 0)[:, None, None, :])
    o_cmp = jnp.einsum('tgrn,ngd->tgrd', p_cmp, vc)
    imp = jnp.einsum('tgrn,ns->tgs', p_cmp, overlap_matrix(n_cmp, n_sel))
    blk = np.arange(n_sel)[None, :]
    cur = (qpos // SEL_BLOCK)[:, None]
    forced = (blk == 0) | (blk == cur) | (blk == cur - 1)
    valid = blk * SEL_BLOCK <= qpos[:, None]
    imp = jnp.where(forced[:, None, :], FORCE_SCORE, imp)
    imp = jnp.where(valid[:, None, :], imp, -jnp.inf)
    n_top = min(TOP_N, n_sel)
    _, top = lax.top_k(imp, n_top)
    g_idx = np.arange(N_KV_A)[None, :, None]
    k_sel = sel_blocks[0][g_idx, top]
    v_sel = sel_blocks[1][g_idx, top]
    kpos = top[..., None] * SEL_BLOCK + np.arange(SEL_BLOCK)
    d_sel = qpos[:, None, None, None] - kpos
    s = jnp.einsum('tgrd,tgnjd->tgrnj', qg, k_sel) * ATTN_SCALE
    s = s - slopes[None, :, :, None, None] * d_sel[:, :, None].astype(jnp.float32)
    s = s.reshape(Tq, N_KV_A, GROUP_A, n_top * SEL_BLOCK)
    p_sel = masked_softmax(s, (d_sel >= 0).reshape(Tq, N_KV_A, 1, n_top * SEL_BLOCK))
    o_sel = jnp.einsum('tgrnj,tgnjd->tgrd', p_sel.reshape(Tq, N_KV_A, GROUP_A, n_top, SEL_BLOCK), v_sel)
    d_win = qpos[:, None] - kwpos[None, :]
    s = jnp.einsum('tgrd,lgd->tgrl', qg, kv_win[:, 0]) * ATTN_SCALE
    s = s - slopes[None, :, :, None] * d_win[:, None, None, :].astype(jnp.float32)
    win_mask = (d_win >= 0) & (d_win < WINDOW) & (kwpos[None, :] >= 0)
    p_win = masked_softmax(s, win_mask[:, None, None, :])
    o_win = jnp.einsum('tgrl,lgd->tgrd', p_win, kv_win[:, 1])
    g = gates.reshape(Tq, 3, N_KV_A, GROUP_A)[..., None]
    o = g[:, 0] * o_cmp + g[:, 1] * o_sel + g[:, 2] * o_win
    return o.reshape(Tq, N_HEADS_A * HEAD_DIM)


def nsa_prompt_seq(h, params):
    T = h.shape[0]
    q, kv_cmp, kv_sel, kv_win, gates = nsa_project(h, params)
    kc, vc, sel_blocks = nsa_global_keys(kv_cmp, kv_sel, params)
    kv_win_pad = jnp.pad(kv_win, ((WINDOW, 0), (0, 0), (0, 0), (0, 0)))

    def block(b):
        qs = b * Q_BLOCK
        qpos = qs + jnp.arange(Q_BLOCK, dtype=jnp.int32)
        qb = lax.dynamic_slice_in_dim(q, qs, Q_BLOCK, 0)
        gb = lax.dynamic_slice_in_dim(gates, qs, Q_BLOCK, 0)
        kwb = lax.dynamic_slice_in_dim(kv_win_pad, qs, WINDOW + Q_BLOCK, 0)
        kwpos = qs - WINDOW + jnp.arange(WINDOW + Q_BLOCK, dtype=jnp.int32)
        return nsa_attend(qb, qpos, gb, kc, vc, sel_blocks, kwb, kwpos)

    o = lax.map(block, jnp.arange(T // Q_BLOCK, dtype=jnp.int32)).reshape(T, D_MODEL)
    return o, kv_cmp, kv_sel, kv_win[T - min(WINDOW, T):]


def nsa_sample_seq(args, cache_cmp, cache_sel, layer, params):
    h, pages, win_buf = args
    T = h.shape[0]
    past = pages.shape[0] * PAGE_SIZE
    q, kv_cmp, kv_sel, kv_win, gates = nsa_project(h, params)
    past_cmp = cache_cmp[layer, pages].reshape(past, 2, N_KV_A, HEAD_DIM)
    past_sel = cache_sel[layer, pages].reshape(past, 2, N_KV_A, HEAD_DIM)
    kc, vc, sel_blocks = nsa_global_keys(jnp.concatenate([past_cmp, kv_cmp], axis=0),
                                         jnp.concatenate([past_sel, kv_sel], axis=0), params)
    wb = win_buf.shape[0]
    win_all = jnp.concatenate([win_buf, kv_win], axis=0)
    kwpos = past - wb + jnp.arange(wb + T, dtype=jnp.int32)
    qpos = past + jnp.arange(T, dtype=jnp.int32)
    o = nsa_attend(q, qpos, gates, kc, vc, sel_blocks, win_all, kwpos)
    return o, kv_cmp, kv_sel, win_all[T:]


def fox_project(h, params):
    w_in, q_norm, k_norm, f_bias = params
    T = h.shape[0]
    proj = h @ w_in
    q = rms_norm(proj[:, :B_HEAD_COLS].reshape(T, N_HEADS_B, HEAD_DIM), q_norm)
    k = rms_norm(proj[:, B_HEAD_COLS:2 * B_HEAD_COLS].reshape(T, N_HEADS_B, HEAD_DIM), k_norm)
    v = proj[:, 2 * B_HEAD_COLS:3 * B_HEAD_COLS].reshape(T, N_HEADS_B, HEAD_DIM)
    logf = jax.nn.log_sigmoid((proj[:, 3 * B_HEAD_COLS:] + f_bias).astype(jnp.float32))
    return q, jnp.stack([k, v], axis=1), logf


def fox_attend(q, qpos, c_q, k, v, kpos, c_k):
    s = jnp.einsum('thd,lhd->htl', q, k) * ATTN_SCALE
    s = s.astype(jnp.float32) + (c_q.T[:, :, None] - c_k.T[:, None, :])
    p = masked_softmax(s, (kpos[None, :] <= qpos[:, None])[None])
    o = jnp.einsum('htl,lhd->thd', p, v)
    return o.reshape(q.shape[0], N_HEADS_B * HEAD_DIM)


def fox_prompt_seq(h, params):
    T = h.shape[0]
    q, kv, logf = fox_project(h, params)
    c = jnp.cumsum(logf, axis=0)
    k, v = kv[:, 0], kv[:, 1]
    kpos = jnp.arange(T, dtype=jnp.int32)

    def block(b):
        qs = b * Q_BLOCK
        qpos = qs + jnp.arange(Q_BLOCK, dtype=jnp.int32)
        return fox_attend(lax.dynamic_slice_in_dim(q, qs, Q_BLOCK, 0), qpos,
                          lax.dynamic_slice_in_dim(c, qs, Q_BLOCK, 0), k, v, kpos, c)

    o = lax.map(block, jnp.arange(T // Q_BLOCK, dtype=jnp.int32)).reshape(T, D_MODEL)
    return o, kv, logf


def fox_sample_seq(args, cache_kv, cache_logf, layer, params):
    h, pages = args
    T = h.shape[0]
    past = pages.shape[0] * PAGE_SIZE
    q, kv, logf = fox_project(h, params)
    past_kv = cache_kv[layer, pages].reshape(past, 2, N_HEADS_B, HEAD_DIM)
    past_lf = cache_logf[layer, pages].reshape(past, N_HEADS_B).astype(jnp.float32)
    kv_all = jnp.concatenate([past_kv, kv], axis=0)
    c = jnp.cumsum(jnp.concatenate([past_lf, logf], axis=0), axis=0)
    kpos = jnp.arange(past + T, dtype=jnp.int32)
    qpos = past + jnp.arange(T, dtype=jnp.int32)
    o = fox_attend(q, qpos, c[past:], kv_all[:, 0], kv_all[:, 1], kpos, c)
    return o, kv, logf


def setup_inputs(seed: int = 0) -> dict:
    key = jax.random.key(seed)
    keys = jax.random.split(key, 40)
    f32 = jnp.float32

    def normal(k, shape, scale=1.0):
        return scale * jax.random.normal(k, shape, f32)

    def gain(k, shape):
        return 1.0 + 0.05 * jax.random.normal(k, shape, f32)

    n_pages = PAST_LEN // PAGE_SIZE
    n_used = DEC_BATCH * n_pages
    n_pool = n_used + n_used // 4
    win_buf = min(WINDOW, PAST_LEN)
    la, lb = N_LAYERS_A, N_LAYERS_B
    page_table = jax.random.permutation(keys[0], n_pool)[:n_used].astype(jnp.int32).reshape(DEC_BATCH, n_pages)
    return {
        'x_prompt': normal(keys[1], (BATCH, SEQ, D_MODEL)),
        'x_sample': normal(keys[2], (DEC_BATCH, DEC_SEQ, D_MODEL)),
        'cache_a_cmp_kv': normal(keys[3], (la, n_pool, PAGE_SIZE, 2, N_KV_A, HEAD_DIM)),
        'cache_a_sel_kv': normal(keys[4], (la, n_pool, PAGE_SIZE, 2, N_KV_A, HEAD_DIM)),
        'cache_a_win_kv': normal(keys[5], (la, DEC_BATCH, win_buf, 2, N_KV_A, HEAD_DIM)),
        'cache_b_kv': normal(keys[6], (lb, n_pool, PAGE_SIZE, 2, N_HEADS_B, HEAD_DIM)),
        'cache_b_logf': jax.nn.log_sigmoid(FORGET_BIAS_INIT + normal(keys[7], (lb, n_pool, PAGE_SIZE, N_HEADS_B))),
        'page_table': page_table,
        'norm_mix': gain(keys[8], (DEPTH, D_MODEL)),
        'norm_ffn': gain(keys[9], (DEPTH, D_MODEL)),
        'a_w_in': normal(keys[10], (la, D_MODEL, A_IN_COLS), D_MODEL ** -0.5),
        'a_q_norm': gain(keys[11], (la, HEAD_DIM)),
        'a_k_norm': gain(keys[12], (la, 3, HEAD_DIM)),
        'a_gate_b': normal(keys[13], (la, 3 * N_HEADS_A), 0.02),
        'a_cmp_pos': normal(keys[14], (la, 2, CMP_BLOCK, HEAD_DIM), 0.1),
        'a_cmp_w1': normal(keys[15], (la, 2, CMP_BLOCK * HEAD_DIM, CMP_HIDDEN), (CMP_BLOCK * HEAD_DIM) ** -0.5),
        'a_cmp_w2': normal(keys[16], (la, 2, CMP_HIDDEN, HEAD_DIM), CMP_HIDDEN ** -0.5),
        'a_w_out': normal(keys[17], (la, D_MODEL, D_MODEL), D_MODEL ** -0.5),
        'b_w_in': normal(keys[18], (lb, D_MODEL, B_IN_COLS), D_MODEL ** -0.5),
        'b_q_norm': gain(keys[19], (lb, HEAD_DIM)),
        'b_k_norm': gain(keys[20], (lb, HEAD_DIM)),
        'b_f_bias': FORGET_BIAS_INIT + normal(keys[21], (lb, N_HEADS_B), 0.3),
        'b_w_out': normal(keys[22], (lb, D_MODEL, D_MODEL), D_MODEL ** -0.5),
        'f_w_gate': normal(keys[23], (la, D_MODEL, D_FF), D_MODEL ** -0.5),
        'f_w_up': normal(keys[24], (la, D_MODEL, D_FF), D_MODEL ** -0.5),
        'f_w_down': normal(keys[25], (la, D_FF, D_MODEL), D_FF ** -0.5),
        'm_router': normal(keys[26], (lb, D_MODEL, N_EXPERTS), D_MODEL ** -0.5),
        'm_w_gate': normal(keys[27], (lb, N_EXPERTS, D_MODEL, D_FF), D_MODEL ** -0.5),
        'm_w_up': normal(keys[28], (lb, N_EXPERTS, D_MODEL, D_FF), D_MODEL ** -0.5),
        'm_w_down': normal(keys[29], (lb, N_EXPERTS, D_FF, D_MODEL), D_FF ** -0.5),
    }


def reference(x_prompt, x_sample, cache_a_cmp_kv, cache_a_sel_kv, cache_a_win_kv, cache_b_kv, cache_b_logf,
              page_table, norm_mix, norm_ffn, a_w_in, a_q_norm, a_k_norm, a_gate_b, a_cmp_pos, a_cmp_w1,
              a_cmp_w2, a_w_out, b_w_in, b_q_norm, b_k_norm, b_f_bias, b_w_out, f_w_gate, f_w_up, f_w_down,
              m_router, m_w_gate, m_w_up, m_w_down):
    xp, xs = x_prompt, x_sample
    a_cmp_p, a_cmp_s, a_sel_p, a_sel_s, a_win_p, a_win_s = [], [], [], [], [], []
    b_kv_p, b_kv_s, b_lf_p, b_lf_s = [], [], [], []
    for i in range(DEPTH):
        j = i // 2
        hp = rms_norm(xp, norm_mix[i])
        hs = rms_norm(xs, norm_mix[i])
        if i % 2 == 0:
            pa = (a_w_in[j], a_q_norm[j], a_k_norm[j], a_gate_b[j], a_cmp_pos[j], a_cmp_w1[j], a_cmp_w2[j])
            op, kc_p, ks_p, kw_p = lax.map(functools.partial(nsa_prompt_seq, params=pa), hp)
            os_, kc_s, ks_s, kw_s = lax.map(
                functools.partial(nsa_sample_seq, cache_cmp=cache_a_cmp_kv, cache_sel=cache_a_sel_kv,
                                  layer=j, params=pa),
                (hs, page_table, cache_a_win_kv[j]))
            a_cmp_p.append(kc_p)
            a_cmp_s.append(kc_s)
            a_sel_p.append(ks_p)
            a_sel_s.append(ks_s)
            a_win_p.append(kw_p)
            a_win_s.append(kw_s)
            mp = op @ a_w_out[j]
            ms = os_ @ a_w_out[j]
        else:
            pb = (b_w_in[j], b_q_norm[j], b_k_norm[j], b_f_bias[j])
            op, kv_p, lf_p = lax.map(functools.partial(fox_prompt_seq, params=pb), hp)
            os_, kv_s, lf_s = lax.map(
                functools.partial(fox_sample_seq, cache_kv=cache_b_kv, cache_logf=cache_b_logf,
                                  layer=j, params=pb),
                (hs, page_table))
            b_kv_p.append(kv_p)
            b_kv_s.append(kv_s)
            b_lf_p.append(lf_p)
            b_lf_s.append(lf_s)
            mp = op @ b_w_out[j]
            ms = os_ @ b_w_out[j]
        xp = xp + mp.astype(xp.dtype)
        xs = xs + ms.astype(xs.dtype)
        hp = rms_norm(xp, norm_ffn[i])
        hs = rms_norm(xs, norm_ffn[i])
        if i % 2 == 0:
            fp = swiglu(hp, f_w_gate[j], f_w_up[j], f_w_down[j])
            fs = swiglu(hs, f_w_gate[j], f_w_up[j], f_w_down[j])
        else:
            fp = moe_ffn(hp, m_router[j], m_w_gate[j], m_w_up[j], m_w_down[j])
            fs = moe_ffn(hs, m_router[j], m_w_gate[j], m_w_up[j], m_w_down[j])
        xp = xp + fp.astype(xp.dtype)
        xs = xs + fs.astype(xs.dtype)
    return (xp, xs,
            jnp.stack(a_cmp_p), jnp.stack(a_cmp_s),
            jnp.stack(a_sel_p), jnp.stack(a_sel_s),
            jnp.stack(a_win_p), jnp.stack(a_win_s),
            jnp.stack(b_kv_p), jnp.stack(b_kv_s),
            jnp.stack(b_lf_p), jnp.stack(b_lf_s))
```

```python
import functools

import numpy as np
import jax
import jax.numpy as jnp
from jax import lax
from jax.experimental import pallas as pl
from jax.experimental.pallas import tpu as pltpu

F32 = jnp.float32
BF16 = jnp.bfloat16

HEAD_DIM = 64
N_HEADS = 16
N_KV_A = 4
GROUP_A = N_HEADS // N_KV_A
CMP_BLOCK = 32
CMP_STRIDE = 16
CMP_HIDDEN = 256
SEL_BLOCK = 64
TOP_N = 16
WINDOW = 512
N_EXPERTS = 8
PAGE = 128
RMS_EPS = 1e-6
FORCE_SCORE = 1e4
ATTN_SCALE = HEAD_DIM ** -0.5
LANES = 128
SEG_CHUNK = 256

NEG = -1e30
PICKED = -3e38
VMEM_LIMIT = 56 * 1024 * 1024


def _alibi_slopes():
    return (2.0 ** (-8.0 * np.arange(1, N_HEADS + 1) / N_HEADS)).astype(np.float32)


def _dot(a, b):
    return jnp.dot(a, b, preferred_element_type=F32)


def _dot_nt(a, b):
    return lax.dot_general(a, b, (((1,), (1,)), ((), ())), preferred_element_type=F32)


def _dot3(x, m):
    x1 = x.astype(BF16)
    r1 = x - x1.astype(F32)
    x2 = r1.astype(BF16)
    x3 = (r1 - x2.astype(F32)).astype(BF16)
    return _dot(x1, m) + _dot(x2, m) + _dot(x3, m)


def _rms(x, g):
    y = x * lax.rsqrt(jnp.mean(x * x, axis=-1, keepdims=True) + RMS_EPS)
    return y * g


def _seg_norm(y, bd, g):
    outs = []
    for c in range(y.shape[1] // SEG_CHUNK):
        yc = y[:, c * SEG_CHUNK:(c + 1) * SEG_CHUNK]
        ms = _dot3(yc * yc, bd)
        outs.append(yc * lax.rsqrt(ms + RMS_EPS))
    out = outs[0] if len(outs) == 1 else jnp.concatenate(outs, axis=1)
    return out * g


def _silu(x):
    return x * jax.nn.sigmoid(x)


def _cparams(sem):
    return pltpu.CompilerParams(dimension_semantics=sem, vmem_limit_bytes=VMEM_LIMIT)


def _seg_matrix():
    i = np.arange(SEG_CHUNK)
    m = (i[:, None] // HEAD_DIM == i[None, :] // HEAD_DIM).astype(np.float32) / HEAD_DIM
    return jnp.asarray(m, BF16)


def _proj_a_kernel(x_ref, g_ref, wq_ref, wkv_ref, wg_ref, bd_ref, qn_ref, kn1_ref, kn2_ref, gb_ref,
                   q_out, cmp_out, sel_out, win_out, gate_out):
    h = _rms(x_ref[...], g_ref[...]).astype(BF16)
    bd = bd_ref[...]
    q_out[...] = _seg_norm(_dot(h, wq_ref[...]), bd, qn_ref[...])
    kv = _dot(h, wkv_ref[...])
    kvw = 2 * N_KV_A * HEAD_DIM
    kw = N_KV_A * HEAD_DIM
    cmp_out[...] = kv[:, :kvw]
    sel_out[:, :kw] = _seg_norm(kv[:, kvw:kvw + kw], bd, kn1_ref[...])
    sel_out[:, kw:] = kv[:, kvw + kw:2 * kvw]
    win_out[:, :kw] = _seg_norm(kv[:, 2 * kvw:2 * kvw + kw], bd, kn2_ref[...])
    win_out[:, kw:] = kv[:, 2 * kvw + kw:3 * kvw]
    gate_out[...] = jax.nn.sigmoid(_dot(h, wg_ref[...]) + gb_ref[...])


def _proj_a(x, g, wq, wkv, wg, bd, qn, kn1, kn2, gb, tm):
    t, d = x.shape
    kvw = 2 * N_KV_A * HEAD_DIM
    const = lambda shape: pl.BlockSpec(shape, lambda i: (0, 0))
    row = lambda w: pl.BlockSpec((tm, w), lambda i: (i, 0))
    return pl.pallas_call(
        _proj_a_kernel,
        grid=(t // tm,),
        in_specs=[row(d), const((1, d)), const(wq.shape), const(wkv.shape), const(wg.shape),
                  const(bd.shape), const(qn.shape), const(kn1.shape), const(kn2.shape), const(gb.shape)],
        out_specs=[row(d), row(kvw), row(kvw), row(kvw), row(LANES)],
        out_shape=[jax.ShapeDtypeStruct((t, d), F32)] + [jax.ShapeDtypeStruct((t, kvw), F32)] * 3
                  + [jax.ShapeDtypeStruct((t, LANES), F32)],
        compiler_params=_cparams(("parallel",)),
        name="proj_a",
    )(x, g, wq, wkv, wg, bd, qn, kn1, kn2, gb)


def _proj_b_kernel(x_ref, g_ref, wq_ref, wk_ref, wv_ref, wf_ref, bd_ref, qn_ref, kn_ref, fb_ref,
                   q_out, kv_out, lf_out):
    h = _rms(x_ref[...], g_ref[...]).astype(BF16)
    bd = bd_ref[...]
    d = q_out.shape[1]
    q_out[...] = _seg_norm(_dot(h, wq_ref[...]), bd, qn_ref[...])
    kv_out[:, :d] = _seg_norm(_dot(h, wk_ref[...]), bd, kn_ref[...])
    kv_out[:, d:] = _dot(h, wv_ref[...])
    z = _dot(h, wf_ref[...]) + fb_ref[...]
    lf_out[...] = jnp.minimum(z, 0.0) - jnp.log1p(jnp.exp(-jnp.abs(z)))


def _proj_b(x, g, wq, wk, wv, wf, bd, qn, kn, fb, tm):
    t, d = x.shape
    const = lambda shape: pl.BlockSpec(shape, lambda i: (0, 0))
    row = lambda w: pl.BlockSpec((tm, w), lambda i: (i, 0))
    return pl.pallas_call(
        _proj_b_kernel,
        grid=(t // tm,),
        in_specs=[row(d), const((1, d)), const(wq.shape), const(wk.shape), const(wv.shape), const(wf.shape),
                  const(bd.shape), const(qn.shape), const(kn.shape), const(fb.shape)],
        out_specs=[row(d), row(2 * d), row(LANES)],
        out_shape=[jax.ShapeDtypeStruct((t, d), F32), jax.ShapeDtypeStruct((t, 2 * d), F32),
                   jax.ShapeDtypeStruct((t, LANES), F32)],
        compiler_params=_cparams(("parallel",)),
        name="proj_b",
    )(x, g, wq, wk, wv, wf, bd, qn, kn, fb)


def _resid_proj_kernel(x_ref, a_ref, w_ref, o_ref):
    o_ref[...] = x_ref[...] + _dot(a_ref[...].astype(BF16), w_ref[...])


def _resid_proj(x, a, w, tm):
    t, d = x.shape
    row = pl.BlockSpec((tm, d), lambda i: (i, 0))
    return pl.pallas_call(
        _resid_proj_kernel,
        grid=(t // tm,),
        in_specs=[row, row, pl.BlockSpec(w.shape, lambda i: (0, 0))],
        out_specs=row,
        out_shape=jax.ShapeDtypeStruct((t, d), F32),
        compiler_params=_cparams(("parallel",)),
        name="resid_proj",
    )(x, a, w)


def _ffn_kernel(x_ref, g_ref, wg_ref, wu_ref, wd_ref, o_ref, xn_ref, acc_ref):
    j = pl.program_id(1)

    @pl.when(j == 0)
    def _():
        xn_ref[...] = _rms(x_ref[...], g_ref[...]).astype(BF16)
        acc_ref[...] = jnp.zeros_like(acc_ref)

    xn = xn_ref[...]
    hmid = _silu(_dot(xn, wg_ref[...])) * _dot(xn, wu_ref[...])
    acc_ref[...] += _dot(hmid.astype(BF16), wd_ref[...])

    @pl.when(j == pl.num_programs(1) - 1)
    def _():
        o_ref[...] = x_ref[...] + acc_ref[...]


def _ffn(x, g, wg, wu, wd, tm, tf):
    t, d = x.shape
    dff = wg.shape[1]
    row = pl.BlockSpec((tm, d), lambda i, j: (i, 0))
    return pl.pallas_call(
        _ffn_kernel,
        grid=(t // tm, dff // tf),
        in_specs=[row, pl.BlockSpec((1, d), lambda i, j: (0, 0)),
                  pl.BlockSpec((d, tf), lambda i, j: (0, j)),
                  pl.BlockSpec((d, tf), lambda i, j: (0, j)),
                  pl.BlockSpec((tf, d), lambda i, j: (j, 0))],
        out_specs=row,
        out_shape=jax.ShapeDtypeStruct((t, d), F32),
        scratch_shapes=[pltpu.VMEM((tm, d), BF16), pltpu.VMEM((tm, d), F32)],
        compiler_params=_cparams(("parallel", "arbitrary")),
        name="ffn",
    )(x, g, wg, wu, wd)


def _moe_kernel(x_ref, g_ref, r_ref, wg_ref, wu_ref, wd_ref, o_ref, xn_ref, gate_ref, acc_ref, eacc_ref):
    e = pl.program_id(1)
    j = pl.program_id(2)
    last_j = pl.num_programs(2) - 1

    @pl.when((e == 0) & (j == 0))
    def _():
        xn = _rms(x_ref[...], g_ref[...]).astype(BF16)
        xn_ref[...] = xn
        acc_ref[...] = jnp.zeros_like(acc_ref)
        logits = _dot(xn, r_ref[...])
        col = lax.broadcasted_iota(jnp.int32, logits.shape, 1).astype(F32)
        real = col < N_EXPERTS
        logits = jnp.where(real, logits, NEG)
        mx = jnp.max(logits, axis=-1, keepdims=True)
        pe = jnp.where(real, jnp.exp(logits - mx), 0.0)
        probs = pe / jnp.sum(pe, axis=-1, keepdims=True)
        work = jnp.where(real, probs, PICKED)
        picked = jnp.zeros(logits.shape, jnp.bool_)
        for _ in range(2):
            top = jnp.max(work, axis=-1, keepdims=True)
            idx = jnp.min(jnp.where(work == top, col, float(LANES)), axis=-1, keepdims=True)
            hit = col == idx
            picked = picked | hit
            work = jnp.where(hit, PICKED, work)
        sel = jnp.where(picked, probs, 0.0)
        gate_ref[...] = sel / jnp.sum(sel, axis=-1, keepdims=True)

    @pl.when(j == 0)
    def _():
        eacc_ref[...] = jnp.zeros_like(eacc_ref)

    xn = xn_ref[...]
    hmid = _silu(_dot(xn, wg_ref[...])) * _dot(xn, wu_ref[...])
    eacc_ref[...] += _dot(hmid.astype(BF16), wd_ref[...])

    @pl.when(j == last_j)
    def _():
        gate = gate_ref[...]
        col = lax.broadcasted_iota(jnp.int32, gate.shape, 1)
        gcol = jnp.sum(jnp.where(col == e, gate, 0.0), axis=-1, keepdims=True)
        acc_ref[...] += gcol * eacc_ref[...]

    @pl.when((e == pl.num_programs(1) - 1) & (j == last_j))
    def _():
        o_ref[...] = x_ref[...] + acc_ref[...]


def _moe(x, g, router, wg, wu, wd, tm, tf):
    t, d = x.shape
    ne, _, dff = wg.shape
    row = pl.BlockSpec((tm, d), lambda i, e, j: (i, 0))
    return pl.pallas_call(
        _moe_kernel,
        grid=(t // tm, ne, dff // tf),
        in_specs=[row, pl.BlockSpec((1, d), lambda i, e, j: (0, 0)),
                  pl.BlockSpec(router.shape, lambda i, e, j: (0, 0)),
                  pl.BlockSpec((None, d, tf), lambda i, e, j: (e, 0, j)),
                  pl.BlockSpec((None, d, tf), lambda i, e, j: (e, 0, j)),
                  pl.BlockSpec((None, tf, d), lambda i, e, j: (e, j, 0))],
        out_specs=row,
        out_shape=jax.ShapeDtypeStruct((t, d), F32),
        scratch_shapes=[pltpu.VMEM((tm, d), BF16), pltpu.VMEM((tm, LANES), F32),
                        pltpu.VMEM((tm, d), F32), pltpu.VMEM((tm, d), F32)],
        compiler_params=_cparams(("parallel", "arbitrary", "arbitrary")),
        name="moe",
    )(x, g, router, wg, wu, wd)


ROWS_PER_PAGE = PAGE // CMP_STRIDE


def _compress_kernel(pt_ref, x_hbm, pos_ref, w1_ref, w2_ref, kn_ref, o_ref, xbuf, sem, *, n_pages, n_kvg):
    b = pl.program_id(0)
    c = pl.program_id(1)
    step = b * n_kvg + c
    n_steps = pl.num_programs(0) * n_kvg
    slot = step % 2

    def page_copy(bb, cc, sl, p):
        return pltpu.make_async_copy(
            x_hbm.at[pt_ref[bb, p], cc],
            xbuf.at[sl, pl.ds(p * ROWS_PER_PAGE, ROWS_PER_PAGE), :],
            sem.at[sl])

    def fetch(bb, cc, sl):
        def body(p, carry):
            page_copy(bb, cc, sl, p).start()
            return carry
        lax.fori_loop(0, n_pages, body, 0)

    @pl.when(step == 0)
    def _():
        fetch(b, c, slot)

    @pl.when(step + 1 < n_steps)
    def _():
        nxt = step + 1
        fetch(nxt // n_kvg, nxt % n_kvg, 1 - slot)

    def wait_body(p, carry):
        page_copy(b, c, slot, p).wait()
        return carry
    lax.fori_loop(0, n_pages, wait_body, 0)

    x = xbuf[slot]
    n_rows = x.shape[0]
    z0 = _dot((x + pos_ref[0]).astype(BF16), w1_ref[0])
    z1 = _dot((x + pos_ref[1]).astype(BF16), w1_ref[1])
    hid = z0 + pltpu.roll(z1, n_rows - 1, 0)
    y = _dot(_silu(hid).astype(BF16), w2_ref[...])
    yn = _rms(y, kn_ref[...])
    is_key = c < (n_kvg // 2)
    o_ref[...] = jnp.where(is_key, yn, y)


def _compress(pt, xr, posr, w1r, w2, kn):
    nb, n_pages = pt.shape
    n_kvg = xr.shape[1]
    half = n_kvg // 2
    n_rows = n_pages * ROWS_PER_PAGE
    width = xr.shape[3]
    grid_spec = pltpu.PrefetchScalarGridSpec(
        num_scalar_prefetch=1,
        grid=(nb, n_kvg),
        in_specs=[pl.BlockSpec(memory_space=pl.ANY),
                  pl.BlockSpec((None, 2, 1, width), lambda b, c, pt: (c // half, 0, 0, 0)),
                  pl.BlockSpec((None, 2, width, CMP_HIDDEN), lambda b, c, pt: (c // half, 0, 0, 0)),
                  pl.BlockSpec((None, CMP_HIDDEN, HEAD_DIM), lambda b, c, pt: (c // half, 0, 0)),
                  pl.BlockSpec((1, HEAD_DIM), lambda b, c, pt: (0, 0))],
        out_specs=pl.BlockSpec((None, None, n_rows, HEAD_DIM), lambda b, c, pt: (b, c, 0, 0)),
        scratch_shapes=[pltpu.VMEM((2, n_rows, width), F32), pltpu.SemaphoreType.DMA((2,))],
    )
    return pl.pallas_call(
        functools.partial(_compress_kernel, n_pages=n_pages, n_kvg=n_kvg),
        grid_spec=grid_spec,
        out_shape=jax.ShapeDtypeStruct((nb, n_kvg, n_rows, HEAD_DIM), F32),
        compiler_params=_cparams(("arbitrary", "arbitrary")),
        name="compress",
    )(pt, xr, posr, w1r, w2, kn)


def _cmp_topk_kernel(q_ref, pos_ref, kc_ref, vc_ref, ov_ref, gate_ref, o_ref, sb_ref, *,
                     tq, n_cmp, n_sel, n_top, slopes):
    nc = kc_ref.shape[0]
    ns = ov_ref.shape[1]
    q = q_ref[...]
    pos = pos_ref[...]
    gates = gate_ref[...]
    n_idx = lax.broadcasted_iota(jnp.int32, (1, nc), 1)
    dcmp = pos - (n_idx * CMP_STRIDE + (CMP_BLOCK - 1))
    mask = (dcmp >= 0) & (n_idx < n_cmp)
    dcf = dcmp.astype(F32)
    blk = lax.broadcasted_iota(jnp.int32, (1, ns), 1)
    blkf = blk.astype(F32)
    cur = pos // SEL_BLOCK
    forced = (blk == 0) | (blk == cur) | (blk == cur - 1)
    valid = (blk * SEL_BLOCK <= pos) & (blk < n_sel)
    ov = ov_ref[...]
    for g in range(N_KV_A):
        heads = range(g * GROUP_A, (g + 1) * GROUP_A)
        qg = (jnp.concatenate([q[:, h * HEAD_DIM:(h + 1) * HEAD_DIM] for h in heads], axis=0)
              * ATTN_SCALE).astype(BF16)
        kg = kc_ref[:, g * HEAD_DIM:(g + 1) * HEAD_DIM].astype(BF16)
        vg = vc_ref[:, g * HEAD_DIM:(g + 1) * HEAD_DIM].astype(BF16)
        s = _dot_nt(qg, kg)
        psum = jnp.zeros((tq, nc), F32)
        for r, h in enumerate(heads):
            sr = s[r * tq:(r + 1) * tq] - slopes[h] * dcf
            sr = jnp.where(mask, sr, NEG)
            mx = jnp.max(sr, axis=-1, keepdims=True)
            p = jnp.where(mask, jnp.exp(sr - mx), 0.0)
            p = p / jnp.maximum(jnp.sum(p, axis=-1, keepdims=True), 1e-30)
            o = _dot(p.astype(BF16), vg)
            o_ref[:, h * HEAD_DIM:(h + 1) * HEAD_DIM] = gates[:, h:h + 1] * o
            psum = psum + p
        imp = _dot(psum.astype(BF16), ov)
        imp = jnp.where(forced, FORCE_SCORE, imp)
        work = jnp.where(valid, imp, NEG)
        work = jnp.where(blk < n_sel, work, PICKED)
        picked = jnp.zeros((tq, ns), jnp.bool_)
        for _ in range(n_top):
            top = jnp.max(work, axis=-1, keepdims=True)
            idx = jnp.min(jnp.where(work == top, blkf, float(ns)), axis=-1, keepdims=True)
            hit = blkf == idx
            picked = picked | hit
            work = jnp.where(hit, PICKED, work)
        sb_ref[:, g * ns:(g + 1) * ns] = jnp.where(picked, 0.0, NEG).astype(BF16)


def _cmp_topk(q, pos, kc, vc, ov, gates, tq, tiles_per_seq, n_cmp, n_sel):
    t, d = q.shape
    nc = kc.shape[1]
    ns = ov.shape[1]
    kvw = N_KV_A * HEAD_DIM
    row = lambda w: pl.BlockSpec((tq, w), lambda i: (i, 0))
    seq = pl.BlockSpec((None, nc, kvw), lambda i: (i // tiles_per_seq, 0, 0))
    slopes = tuple(float(s) for s in _alibi_slopes())
    return pl.pallas_call(
        functools.partial(_cmp_topk_kernel, tq=tq, n_cmp=n_cmp, n_sel=n_sel,
                          n_top=min(TOP_N, n_sel), slopes=slopes),
        grid=(t // tq,),
        in_specs=[row(d), row(1), seq, seq, pl.BlockSpec(ov.shape, lambda i: (0, 0)), row(LANES)],
        out_specs=[row(d), row(N_KV_A * ns)],
        out_shape=[jax.ShapeDtypeStruct((t, d), F32), jax.ShapeDtypeStruct((t, N_KV_A * ns), BF16)],
        compiler_params=_cparams(("parallel",)),
        name="cmp_topk",
    )(q, pos, kc, vc, ov, gates)


def _flash_kernel(*refs, kind, n_groups, tq, tk, n_kk, slopes, gate_off):
    if kind == "fox":
        q_ref, kv_ref, c_ref, ct_ref, o_ref, qs_ref, m_ref, l_ref, acc_ref = refs
    elif kind == "sel":
        q_ref, kv_ref, sb_ref, prev_ref, gate_ref, o_ref, qs_ref, m_ref, l_ref, acc_ref = refs
    else:
        q_ref, kv_ref, prev_ref, gate_ref, o_ref, qs_ref, m_ref, l_ref, acc_ref = refs
    per_group = N_HEADS // n_groups
    kw = n_groups * HEAD_DIM
    i = pl.program_id(0)
    kk = pl.program_id(1)
    diag = ((i + 1) * tq - 1) // tk
    if kind == "win":
        kt = diag - (n_kk - 1) + kk
        active = kt >= 0
        last = kk == n_kk - 1
    else:
        kt = kk
        active = kk <= diag
        last = kk == diag

    @pl.when(kk == 0)
    def _():
        m_ref[...] = jnp.full(m_ref.shape, NEG, F32)
        l_ref[...] = jnp.zeros_like(l_ref)
        acc_ref[...] = jnp.zeros_like(acc_ref)
        q = q_ref[...]
        for h in range(N_HEADS):
            qs_ref[h * tq:(h + 1) * tq, :] = (q[:, h * HEAD_DIM:(h + 1) * HEAD_DIM] * ATTN_SCALE).astype(BF16)

    @pl.when(active)
    def _():
        tpos = i * tq + lax.broadcasted_iota(jnp.int32, (tq, 1), 0)
        kpos = kt * tk + lax.broadcasted_iota(jnp.int32, (1, tk), 1)
        dist = tpos - kpos
        if kind == "win":
            keep = (dist >= 0) & (dist < WINDOW)
        else:
            keep = dist >= 0
        distf = dist.astype(F32)
        if kind == "sel":
            ns = sb_ref.shape[1] // n_groups
            expand = (lax.broadcasted_iota(jnp.int32, (ns, 1), 0) == kpos // SEL_BLOCK).astype(BF16)
        for g in range(n_groups):
            rows = slice(g * per_group * tq, (g + 1) * per_group * tq)
            kg = kv_ref[:, g * HEAD_DIM:(g + 1) * HEAD_DIM].astype(BF16)
            vg = kv_ref[:, kw + g * HEAD_DIM:kw + (g + 1) * HEAD_DIM].astype(BF16)
            s_all = _dot_nt(qs_ref[rows, :], kg)
            if kind == "sel":
                selb = _dot(sb_ref[:, g * ns:(g + 1) * ns], expand)
            for r in range(per_group):
                h = g * per_group + r
                hr = slice(h * tq, (h + 1) * tq)
                s = s_all[r * tq:(r + 1) * tq]
                if kind == "fox":
                    s = s + (c_ref[:, h:h + 1] - ct_ref[h:h + 1, :])
                else:
                    s = s - slopes[h] * distf
                    if kind == "sel":
                        s = s + selb
                s = jnp.where(keep, s, NEG)
                m_prev = m_ref[hr, :]
                m_new = jnp.maximum(m_prev, jnp.max(s, axis=-1, keepdims=True))
                alpha = jnp.exp(m_prev - m_new)
                p = jnp.exp(s - m_new)
                l_ref[hr, :] = alpha * l_ref[hr, :] + jnp.sum(p, axis=-1, keepdims=True)
                acc_ref[hr, :] = alpha * acc_ref[hr, :] + _dot(p.astype(BF16), vg)
                m_ref[hr, :] = m_new

    @pl.when(last)
    def _():
        for h in range(N_HEADS):
            hr = slice(h * tq, (h + 1) * tq)
            cols = slice(h * HEAD_DIM, (h + 1) * HEAD_DIM)
            o = acc_ref[hr, :] / l_ref[hr, :]
            if kind == "fox":
                o_ref[:, cols] = o
            else:
                o_ref[:, cols] = prev_ref[:, cols] + gate_ref[:, gate_off + h:gate_off + h + 1] * o


def _flash(kind, q, kv, extras, tq, tk):
    t, d = q.shape
    n_groups = kv.shape[1] // (2 * HEAD_DIM)
    nq = t // tq
    if kind == "win":
        q0 = np.arange(nq) * tq
        n_kk = int(np.max((q0 + tq - 1) // tk - np.maximum(q0 - (WINDOW - 1), 0) // tk + 1))
    else:
        n_kk = t // tk

    def kt_of(i, kk):
        diag = ((i + 1) * tq - 1) // tk
        if kind == "win":
            return jnp.maximum(diag - (n_kk - 1) + kk, 0)
        return jnp.minimum(kk, diag)

    qrow = lambda w: pl.BlockSpec((tq, w), lambda i, kk: (i, 0))
    in_specs = [qrow(d), pl.BlockSpec((tk, kv.shape[1]), lambda i, kk: (kt_of(i, kk), 0))]
    if kind == "fox":
        c, ct = extras
        in_specs += [qrow(c.shape[1]), pl.BlockSpec((ct.shape[0], tk), lambda i, kk: (0, kt_of(i, kk)))]
        gate_off = 0
    elif kind == "sel":
        sb, prev, gates = extras
        in_specs += [qrow(sb.shape[1]), qrow(d), qrow(LANES)]
        gate_off = N_HEADS
    else:
        prev, gates = extras
        in_specs += [qrow(d), qrow(LANES)]
        gate_off = 2 * N_HEADS
    slopes = tuple(float(s) for s in _alibi_slopes())
    return pl.pallas_call(
        functools.partial(_flash_kernel, kind=kind, n_groups=n_groups, tq=tq, tk=tk, n_kk=n_kk,
                          slopes=slopes, gate_off=gate_off),
        grid=(nq, n_kk),
        in_specs=in_specs,
        out_specs=qrow(d),
        out_shape=jax.ShapeDtypeStruct((t, d), F32),
        scratch_shapes=[pltpu.VMEM((N_HEADS * tq, HEAD_DIM), BF16), pltpu.VMEM((N_HEADS * tq, 1), F32),
                        pltpu.VMEM((N_HEADS * tq, 1), F32), pltpu.VMEM((N_HEADS * tq, HEAD_DIM), F32)],
        compiler_params=_cparams(("parallel", "arbitrary")),
        name="flash_" + kind,
    )(q, kv, *extras)


def _cumsum_kernel(x_ref, o_ref, carry_ref):
    @pl.when(pl.program_id(0) == 0)
    def _():
        carry_ref[...] = jnp.zeros_like(carry_ref)

    x = x_ref[...]
    tm = x.shape[1]
    upper = (lax.broadcasted_iota(jnp.int32, (tm, tm), 0)
             <= lax.broadcasted_iota(jnp.int32, (tm, tm), 1)).astype(BF16)
    c = _dot3(x, upper) + carry_ref[:, 0:1]
    o_ref[...] = c
    carry_ref[...] = jnp.broadcast_to(c[:, tm - 1:tm], carry_ref.shape)


def _cumsum_t(xt, tm):
    nh, t = xt.shape
    blk = pl.BlockSpec((nh, tm), lambda i: (0, i))
    return pl.pallas_call(
        _cumsum_kernel,
        grid=(t // tm,),
        in_specs=[blk],
        out_specs=blk,
        out_shape=jax.ShapeDtypeStruct((nh, t), F32),
        scratch_shapes=[pltpu.VMEM((nh, LANES), F32)],
        compiler_params=_cparams(("arbitrary",)),
        name="cumsum",
    )(xt)


def _decode_kernel(*refs, kind, n_pp, n_pages, n_new, past, key_base):
    pt_ref = refs[0]
    qbd_ref, slope_ref, scale_ref, prev_ref, knew_ref, vnew_ref = refs[1:7]
    k = 7
    if kind == "fox":
        lfn_ref = refs[k]
        k += 1
    elif kind == "sel":
        sb_ref = refs[k]
        k += 1
    page_refs = refs[k:k + n_pp]
    k += n_pp
    if kind == "fox":
        lf_refs = refs[k:k + n_pp]
        k += n_pp
    o_ref = refs[k]
    m_ref, l_ref, acc_ref = refs[k + 1:k + 4]
    if kind == "fox":
        carry_ref = refs[k + 4]
    del pt_ref

    s_id = pl.program_id(1)
    n_rows, c_dim = qbd_ref.shape
    kv_heads = c_dim // HEAD_DIM
    row = lax.broadcasted_iota(jnp.int32, (n_rows, 1), 0)
    t_row = row // N_HEADS
    h_row = row % N_HEADS
    lane = lax.broadcasted_iota(jnp.int32, (1, PAGE), 1)
    qbd = qbd_ref[...]

    @pl.when(s_id == 0)
    def _():
        m_ref[...] = jnp.full(m_ref.shape, NEG, F32)
        l_ref[...] = jnp.zeros_like(l_ref)
        acc_ref[...] = jnp.zeros_like(acc_ref)
        if kind == "fox":
            carry_ref[...] = jnp.zeros_like(carry_ref)

    if kind == "fox":
        incl = (lax.broadcasted_iota(jnp.int32, (PAGE, PAGE), 0)
                <= lax.broadcasted_iota(jnp.int32, (PAGE, PAGE), 1)).astype(BF16)
        cnu = _dot3(lfn_ref[...], incl)
        cn = jnp.sum(jnp.where(lane == t_row, cnu, 0.0), axis=-1, keepdims=True)
        after = (lax.broadcasted_iota(jnp.int32, (PAGE, PAGE), 0)
                 > lax.broadcasted_iota(jnp.int32, (PAGE, PAGE), 1)).astype(BF16)
    else:
        slope = slope_ref[...]
        qpos = past + t_row

    def update(s, v_parts):
        m_prev = m_ref[...]
        m_new = jnp.maximum(m_prev, jnp.max(s, axis=-1, keepdims=True))
        alpha = jnp.exp(m_prev - m_new)
        p = jnp.exp(s - m_new)
        l_ref[...] = alpha * l_ref[...] + jnp.sum(p, axis=-1, keepdims=True)
        acc = alpha * acc_ref[...]
        for j, (v, transposed) in enumerate(v_parts):
            pj = p[:, j * PAGE:(j + 1) * PAGE].astype(BF16)
            acc = acc + (_dot_nt(pj, v) if transposed else _dot(pj, v))
        acc_ref[...] = acc
        m_ref[...] = m_new

    s_parts, b_parts, v_parts = [], [], []
    for j in range(n_pp):
        page_idx = n_pages - 1 - (s_id * n_pp + j)
        kt = page_refs[j][0].astype(BF16)
        s_parts.append(_dot(qbd, kt))
        v_parts.append((page_refs[j][1].astype(BF16), True))
        if kind == "fox":
            lf = lf_refs[j][...]
            carry = carry_ref[:, 0:1]
            suf = _dot3(lf, after) + carry
            carry_ref[...] = jnp.broadcast_to(carry + jnp.sum(lf, axis=-1, keepdims=True), carry_ref.shape)
            b_parts.append(jnp.concatenate([suf] * (n_rows // N_HEADS), axis=0) + cn)
        else:
            kbuf = page_idx * PAGE + lane
            bias = -slope * (qpos - (key_base + kbuf)).astype(F32)
            if kind == "win":
                bias = jnp.where(kbuf > t_row, bias, NEG)
            b_parts.append(bias)
    s = jnp.concatenate(s_parts, axis=1) + jnp.concatenate(b_parts, axis=1)
    if kind == "sel":
        sb = sb_ref[...]
        ns = sb.shape[1]
        kpos = jnp.concatenate(
            [(n_pages - 1 - (s_id * n_pp + j)) * PAGE + lane for j in range(n_pp)], axis=1)
        expand = (lax.broadcasted_iota(jnp.int32, (ns, 1), 0) == kpos // SEL_BLOCK).astype(BF16)
        s = s + _dot(sb, expand)
    update(s, v_parts)

    @pl.when(s_id == pl.num_programs(1) - 1)
    def _():
        s_new = _dot_nt(qbd, knew_ref[...].astype(BF16))
        ok = (lane <= t_row) & (lane < n_new)
        if kind == "fox":
            bias = cn - cnu
        else:
            bias = -slope * (t_row - lane).astype(F32)
            if kind == "sel":
                ns = sb_ref.shape[1]
                expand_new = (lax.broadcasted_iota(jnp.int32, (ns, 1), 0)
                              == (past + lane) // SEL_BLOCK).astype(BF16)
                bias = bias + _dot(sb_ref[...], expand_new)
        update(jnp.where(ok, s_new + bias, NEG), [(vnew_ref[...].astype(BF16), False)])
        o_full = acc_ref[...] / l_ref[...]
        col_head = lax.broadcasted_iota(jnp.int32, (1, c_dim), 1) // HEAD_DIM
        own = col_head == h_row // (N_HEADS // kv_heads)
        o_full = jnp.where(own, o_full, 0.0)
        o = o_full[:, 0:HEAD_DIM]
        for c in range(1, kv_heads):
            o = o + o_full[:, c * HEAD_DIM:(c + 1) * HEAD_DIM]
        o_ref[...] = prev_ref[...] + scale_ref[...] * o


def _decode(kind, pt, qbd, slope_rows, scale_rows, prev, knew, vnew, extra, pages, lf_pages,
            n_pp, n_new, past, key_base):
    nb, n_pages = pt.shape
    n_rows, c_dim = qbd.shape[1:]
    n_steps = n_pages // n_pp
    seq = lambda shape: pl.BlockSpec((None,) + shape, lambda b, s, pt: (b,) + (0,) * len(shape))

    def page_map(j):
        return lambda b, s, pt: (pt[b, n_pages - 1 - (s * n_pp + j)], 0, 0, 0)

    def lf_map(j):
        return lambda b, s, pt: (pt[b, n_pages - 1 - (s * n_pp + j)], 0, 0)

    in_specs = [seq((n_rows, c_dim)), pl.BlockSpec((n_rows, 1), lambda b, s, pt: (0, 0)),
                seq((n_rows, 1)), seq((n_rows, HEAD_DIM)), seq((PAGE, c_dim)), seq((PAGE, c_dim))]
    args = [qbd, slope_rows, scale_rows, prev, knew, vnew]
    if extra is not None:
        in_specs.append(seq(extra.shape[1:]))
        args.append(extra)
    in_specs += [pl.BlockSpec((None, 2, c_dim, PAGE), page_map(j)) for j in range(n_pp)]
    args += [pages] * n_pp
    scratch = [pltpu.VMEM((n_rows, 1), F32), pltpu.VMEM((n_rows, 1), F32), pltpu.VMEM((n_rows, c_dim), F32)]
    if kind == "fox":
        in_specs += [pl.BlockSpec((None, N_HEADS, PAGE), lf_map(j)) for j in range(n_pp)]
        args += [lf_pages] * n_pp
        scratch.append(pltpu.VMEM((N_HEADS, LANES), F32))
    grid_spec = pltpu.PrefetchScalarGridSpec(
        num_scalar_prefetch=1, grid=(nb, n_steps), in_specs=in_specs,
        out_specs=seq((n_rows, HEAD_DIM)), scratch_shapes=scratch)
    return pl.pallas_call(
        functools.partial(_decode_kernel, kind=kind, n_pp=n_pp, n_pages=n_pages, n_new=n_new,
                          past=past, key_base=key_base),
        grid_spec=grid_spec,
        out_shape=jax.ShapeDtypeStruct((nb, n_rows, HEAD_DIM), F32),
        compiler_params=_cparams(("parallel", "arbitrary")),
        name="decode_" + kind,
    )(pt, *args)


def _overlap_matrix(n_cmp_pad, n_cmp, n_sel_pad, n_sel):
    cs = np.arange(n_cmp_pad)[:, None] * CMP_STRIDE
    ss = np.arange(n_sel_pad)[None, :] * SEL_BLOCK
    ov = np.minimum(cs + CMP_BLOCK, ss + SEL_BLOCK) - np.maximum(cs, ss)
    ov = np.clip(ov, 0, None) / CMP_BLOCK
    ov = ov * (np.arange(n_cmp_pad)[:, None] < n_cmp) * (np.arange(n_sel_pad)[None, :] < n_sel)
    return jnp.asarray(ov.astype(np.float32), BF16)


def _tile_heads(v, n):
    return jnp.tile(v.astype(F32), n)[None, :]


def _pad_cols(w, width):
    return jnp.pad(w, ((0, 0), (0, width - w.shape[1])))


def _block_diag_q(q, kv_heads):
    nb, t = q.shape[:2]
    own = (np.arange(N_HEADS)[:, None] // (N_HEADS // kv_heads) == np.arange(kv_heads)[None, :])
    qb = q[:, :, :, None, :] * jnp.asarray(own, F32)[None, None, :, :, None]
    return (qb * ATTN_SCALE).reshape(nb, t * N_HEADS, kv_heads * HEAD_DIM).astype(BF16)


def _pages_t(cache):
    pool, rows, two, heads, hd = cache.shape
    return jnp.transpose(cache, (0, 2, 3, 4, 1)).reshape(pool, two, heads * hd, rows)


def _pad_rows(x, rows):
    return jnp.pad(x, ((0, 0), (0, rows - x.shape[1]), (0, 0)))


def _nsa_weights(a_w_in, a_q_norm, a_k_norm, a_gate_b, a_cmp_pos, a_cmp_w1, a_cmp_w2, d):
    qc = N_HEADS * HEAD_DIM
    kvw = 2 * N_KV_A * HEAD_DIM
    w = a_w_in.astype(BF16)
    half = CMP_STRIDE * HEAD_DIM
    return dict(
        wq=w[:, :qc], wkv=w[:, qc:qc + 3 * kvw], wg=_pad_cols(w[:, qc + 3 * kvw:], LANES),
        qn=_tile_heads(a_q_norm, N_HEADS), kn0=a_k_norm[0][None, :].astype(F32),
        kn1=_tile_heads(a_k_norm[1], N_KV_A), kn2=_tile_heads(a_k_norm[2], N_KV_A),
        gb=_pad_cols(a_gate_b[None, :].astype(F32), LANES),
        posr=a_cmp_pos.reshape(2, 2, 1, half).astype(F32),
        w1r=a_cmp_w1.reshape(2, 2, half, CMP_HIDDEN).astype(BF16),
        w2=a_cmp_w2.astype(BF16))


def _row_groups(x, n_kvg):
    rows = x.shape[0]
    r = rows // CMP_STRIDE
    xr = x.reshape(r // ROWS_PER_PAGE, ROWS_PER_PAGE, CMP_STRIDE, n_kvg, HEAD_DIM)
    return jnp.transpose(xr, (0, 3, 1, 2, 4)).reshape(r // ROWS_PER_PAGE, n_kvg, ROWS_PER_PAGE,
                                                      CMP_STRIDE * HEAD_DIM)


def _split_kc_vc(cmp_out):
    nb, _, rows, _ = cmp_out.shape
    both = jnp.transpose(cmp_out.reshape(nb, 2, N_KV_A, rows, HEAD_DIM), (0, 1, 3, 2, 4))
    both = both.reshape(nb, 2, rows, N_KV_A * HEAD_DIM)
    return both[:, 0], both[:, 1]


def _nsa_prompt(xp, g_mix, wts, tm, tq, tk):
    t, d = xp.shape
    bd = _seg_matrix()
    q, kv_cmp, kv_sel, kv_win, gates = _proj_a(xp, g_mix, wts["wq"], wts["wkv"], wts["wg"], bd,
                                               wts["qn"], wts["kn1"], wts["kn2"], wts["gb"], tm)
    n_kvg = 2 * N_KV_A
    xr = _row_groups(kv_cmp, n_kvg)
    pt = jnp.arange(xr.shape[0], dtype=jnp.int32)[None, :]
    kc, vc = _split_kc_vc(_compress(pt, xr, wts["posr"], wts["w1r"], wts["w2"], wts["kn0"]))
    n_cmp = (t - CMP_BLOCK) // CMP_STRIDE + 1
    n_sel = -(-t // SEL_BLOCK)
    ns_pad = -(-n_sel // LANES) * LANES
    ov = _overlap_matrix(kc.shape[1], n_cmp, ns_pad, n_sel)
    pos = jnp.arange(t, dtype=jnp.int32)[:, None]
    o_cmp, sb = _cmp_topk(q, pos, kc, vc, ov, gates, tq, t // tq, n_cmp, n_sel)
    o = _flash("sel", q, kv_sel, (sb, o_cmp, gates), tq, tk)
    o = _flash("win", q, kv_win, (o, gates), tq, tk)
    return o, kv_cmp, kv_sel, kv_win


def _nsa_sample(xs, g_mix, wts, cache_cmp, cache_sel, cache_win, page_table, tm):
    nb, n_new, d = xs.shape
    n_pages = page_table.shape[1]
    past = n_pages * PAGE
    bd = _seg_matrix()
    q, kv_cmp, kv_sel, kv_win, gates = _proj_a(xs.reshape(nb * n_new, d), g_mix, wts["wq"], wts["wkv"],
                                               wts["wg"], bd, wts["qn"], wts["kn1"], wts["kn2"],
                                               wts["gb"], tm)
    n_kvg = 2 * N_KV_A
    kvw = n_kvg * HEAD_DIM
    kw = N_KV_A * HEAD_DIM
    pool = cache_cmp.shape[0]
    xr = _row_groups(cache_cmp.reshape(pool * PAGE, kvw), n_kvg)
    kc, vc = _split_kc_vc(_compress(page_table, xr, wts["posr"], wts["w1r"], wts["w2"], wts["kn0"]))
    total = past + n_new
    n_cmp = (total - CMP_BLOCK) // CMP_STRIDE + 1
    assert n_cmp <= kc.shape[1] and (n_cmp - 1) * CMP_STRIDE + CMP_BLOCK <= past
    n_sel = -(-total // SEL_BLOCK)
    ns_pad = -(-n_sel // LANES) * LANES
    ov = _overlap_matrix(kc.shape[1], n_cmp, ns_pad, n_sel)
    rows_pad = 8
    qp = _pad_rows(q.reshape(nb, n_new, d), rows_pad).reshape(nb * rows_pad, d)
    gp = _pad_rows(gates.reshape(nb, n_new, LANES), rows_pad).reshape(nb * rows_pad, LANES)
    pos = jnp.tile(past + jnp.minimum(jnp.arange(rows_pad, dtype=jnp.int32), n_new - 1), nb)[:, None]
    o_cmp, sb = _cmp_topk(qp, pos, kc, vc, ov, gp, rows_pad, 1, n_cmp, n_sel)
    n_rows = n_new * N_HEADS
    o_cmp = o_cmp.reshape(nb, rows_pad, d)[:, :n_new].reshape(nb, n_rows, HEAD_DIM)
    sb = sb.reshape(nb, rows_pad, N_KV_A, ns_pad)[:, :n_new]
    sb = jnp.repeat(sb, GROUP_A, axis=2).reshape(nb, n_rows, ns_pad)
    qbd = _block_diag_q(q.reshape(nb, n_new, N_HEADS, HEAD_DIM), N_KV_A)
    slope_rows = jnp.asarray(np.tile(_alibi_slopes(), n_new)[:, None])
    g3 = gates.reshape(nb, n_new, LANES)
    scale_sel = g3[:, :, N_HEADS:2 * N_HEADS].reshape(nb, n_rows, 1)
    scale_win = g3[:, :, 2 * N_HEADS:3 * N_HEADS].reshape(nb, n_rows, 1)
    new_sel = kv_sel.reshape(nb, n_new, kvw)
    o = _decode("sel", page_table, qbd, slope_rows, scale_sel, o_cmp,
                _pad_rows(new_sel[:, :, :kw], PAGE), _pad_rows(new_sel[:, :, kw:], PAGE), sb,
                _pages_t(cache_sel), None, n_pp=min(16, n_pages), n_new=n_new, past=past, key_base=0)
    wb = cache_win.shape[1]
    assert wb == WINDOW
    win_pages = wb // PAGE
    win_t = _pages_t(cache_win.reshape(nb * win_pages, PAGE, 2, N_KV_A, HEAD_DIM))
    win_pt = jnp.arange(nb * win_pages, dtype=jnp.int32).reshape(nb, win_pages)
    new_win = kv_win.reshape(nb, n_new, kvw)
    o = _decode("win", win_pt, qbd, slope_rows, scale_win, o,
                _pad_rows(new_win[:, :, :kw], PAGE), _pad_rows(new_win[:, :, kw:], PAGE), None,
                win_t, None, n_pp=win_pages, n_new=n_new, past=past, key_base=past - wb)
    shape5 = (nb, n_new, 2, N_KV_A, HEAD_DIM)
    win_all = jnp.concatenate([cache_win, kv_win.reshape(shape5)], axis=1)[:, n_new:]
    return (o.reshape(nb * n_new, d), kv_cmp.reshape(shape5), kv_sel.reshape(shape5), win_all)


def _fox_weights(b_w_in, b_q_norm, b_k_norm, b_f_bias):
    d = N_HEADS * HEAD_DIM
    w = b_w_in.astype(BF16)
    return dict(wq=w[:, :d], wk=w[:, d:2 * d], wv=w[:, 2 * d:3 * d], wf=_pad_cols(w[:, 3 * d:], LANES),
                qn=_tile_heads(b_q_norm, N_HEADS), kn=_tile_heads(b_k_norm, N_HEADS),
                fb=_pad_cols(b_f_bias[None, :].astype(F32), LANES))


def _fox_prompt(xp, g_mix, wts, tm, tq, tk):
    bd = _seg_matrix()
    q, kv, lf = _proj_b(xp, g_mix, wts["wq"], wts["wk"], wts["wv"], wts["wf"], bd,
                        wts["qn"], wts["kn"], wts["fb"], tm)
    logf = lf[:, :N_HEADS]
    ct = _cumsum_t(logf.T, min(512, xp.shape[0]))
    o = _flash("fox", q, kv, (ct.T, ct), tq, tk)
    return o, kv, logf


def _fox_sample(xs, g_mix, wts, cache_kv, cache_logf, page_table, tm):
    nb, n_new, d = xs.shape
    n_pages = page_table.shape[1]
    past = n_pages * PAGE
    bd = _seg_matrix()
    q, kv, lf = _proj_b(xs.reshape(nb * n_new, d), g_mix, wts["wq"], wts["wk"], wts["wv"], wts["wf"], bd,
                        wts["qn"], wts["kn"], wts["fb"], tm)
    logf = lf[:, :N_HEADS]
    n_rows = n_new * N_HEADS
    qbd = _block_diag_q(q.reshape(nb, n_new, N_HEADS, HEAD_DIM), N_HEADS)
    lfn = jnp.transpose(logf.reshape(nb, n_new, N_HEADS), (0, 2, 1))
    lfn = jnp.tile(jnp.pad(lfn, ((0, 0), (0, 0), (0, PAGE - n_new))), (1, n_new, 1))
    ones = jnp.ones((nb, n_rows, 1), F32)
    zeros = jnp.zeros((nb, n_rows, HEAD_DIM), F32)
    new_kv = kv.reshape(nb, n_new, 2 * d)
    o = _decode("fox", page_table, qbd, jnp.zeros((n_rows, 1), F32), ones, zeros,
                _pad_rows(new_kv[:, :, :d], PAGE), _pad_rows(new_kv[:, :, d:], PAGE), lfn,
                _pages_t(cache_kv), jnp.transpose(cache_logf, (0, 2, 1)),
                n_pp=min(8, n_pages), n_new=n_new, past=past, key_base=0)
    return (o.reshape(nb * n_new, d), kv.reshape(nb, n_new, 2, N_HEADS, HEAD_DIM),
            logf.reshape(nb, n_new, N_HEADS))


def kernel(x_prompt, x_sample, cache_a_cmp_kv, cache_a_sel_kv, cache_a_win_kv, cache_b_kv, cache_b_logf,
           page_table, norm_mix, norm_ffn, a_w_in, a_q_norm, a_k_norm, a_gate_b, a_cmp_pos, a_cmp_w1,
           a_cmp_w2, a_w_out, b_w_in, b_q_norm, b_k_norm, b_f_bias, b_w_out, f_w_gate, f_w_up, f_w_down,
           m_router, m_w_gate, m_w_up, m_w_down):
    batch, seq, d = x_prompt.shape
    nb, n_new, _ = x_sample.shape
    assert batch == 1
    ts = nb * n_new
    tm_p = min(512, seq)
    tq = min(256, seq)
    tk = min(512, seq)
    tm_f = min(1024, seq)
    tf = 256
    xp = x_prompt.reshape(seq, d)
    xs = x_sample.reshape(ts, d)
    gm = norm_mix.astype(F32)[:, None, :]
    gf = norm_ffn.astype(F32)[:, None, :]

    wa = _nsa_weights(a_w_in[0], a_q_norm[0], a_k_norm[0], a_gate_b[0], a_cmp_pos[0], a_cmp_w1[0],
                      a_cmp_w2[0], d)
    op, cmp_p, sel_p, win_p = _nsa_prompt(xp, gm[0], wa, tm_p, tq, tk)
    os_, cmp_s, sel_s, win_s = _nsa_sample(x_sample, gm[0], wa, cache_a_cmp_kv[0], cache_a_sel_kv[0],
                                           cache_a_win_kv[0], page_table, ts)
    w_out = a_w_out[0].astype(BF16)
    xp = _resid_proj(xp, op, w_out, tm_p)
    xs = _resid_proj(xs, os_, w_out, ts)
    fg, fu, fd = f_w_gate[0].astype(BF16), f_w_up[0].astype(BF16), f_w_down[0].astype(BF16)
    xp = _ffn(xp, gf[0], fg, fu, fd, tm_f, tf)
    xs = _ffn(xs, gf[0], fg, fu, fd, ts, tf)

    wb = _fox_weights(b_w_in[0], b_q_norm[0], b_k_norm[0], b_f_bias[0])
    op, kv_p, lf_p = _fox_prompt(xp, gm[1], wb, tm_p, tq, tk)
    os_, kv_s, lf_s = _fox_sample(xs.reshape(nb, n_new, d), gm[1], wb, cache_b_kv[0], cache_b_logf[0],
                                  page_table, ts)
    w_out = b_w_out[0].astype(BF16)
    xp = _resid_proj(xp, op, w_out, tm_p)
    xs = _resid_proj(xs, os_, w_out, ts)
    router = _pad_cols(m_router[0].astype(BF16), LANES)
    mg, mu, md = m_w_gate[0].astype(BF16), m_w_up[0].astype(BF16), m_w_down[0].astype(BF16)
    xp = _moe(xp, gf[1], router, mg, mu, md, tm_f, tf)
    xs = _moe(xs, gf[1], router, mg, mu, md, ts, tf)

    wlen = min(WINDOW, seq)
    kv5 = lambda a, rows: a.reshape(1, 1, rows, 2, N_KV_A, HEAD_DIM)
    return (xp.reshape(batch, seq, d), xs.reshape(nb, n_new, d),
            kv5(cmp_p, seq), cmp_s[None],
            kv5(sel_p, seq), sel_s[None],
            kv5(win_p[seq - wlen:], wlen), win_s[None],
            kv_p.reshape(1, 1, seq, 2, N_HEADS, HEAD_DIM), kv_s[None],
            lf_p.reshape(1, 1, seq, N_HEADS), lf_s[None])
```

```python
import functools

import numpy as np
import jax
import jax.numpy as jnp
from jax import lax
from jax.experimental import pallas as pl
from jax.experimental.pallas import tpu as pltpu

F32 = jnp.float32
BF16 = jnp.bfloat16

HEAD_DIM = 64
N_HEADS = 16
N_KV_A = 4
GROUP_A = N_HEADS // N_KV_A
CMP_BLOCK = 32
CMP_STRIDE = 16
CMP_HIDDEN = 256
SEL_BLOCK = 64
TOP_N = 16
WINDOW = 512
N_EXPERTS = 8
PAGE = 128
RMS_EPS = 1e-6
FORCE_SCORE = 1e4
ATTN_SCALE = HEAD_DIM ** -0.5
LOG2E = 1.4426950408889634
LANES = 128
SEG_CHUNK = 256
ROW_CHUNK = 64

NEG = -1e30
PICKED = -3e38
VMEM_LIMIT = 56 * 1024 * 1024


def _alibi_slopes():
    return (2.0 ** (-8.0 * np.arange(1, N_HEADS + 1) / N_HEADS)).astype(np.float32)


def _slopes_log2():
    return tuple(float(np.float32(s) * np.float32(LOG2E)) for s in _alibi_slopes())


def _dot(a, b):
    return jnp.dot(a, b, preferred_element_type=F32)


def _dot_nt(a, b):
    return lax.dot_general(a, b, (((1,), (1,)), ((), ())), preferred_element_type=F32)


def _dot3(x, m):
    x1 = x.astype(BF16)
    r1 = x - x1.astype(F32)
    x2 = r1.astype(BF16)
    x3 = (r1 - x2.astype(F32)).astype(BF16)
    return _dot(x1, m) + _dot(x2, m) + _dot(x3, m)


def _rms(x, g):
    y = x * lax.rsqrt(jnp.mean(x * x, axis=-1, keepdims=True) + RMS_EPS)
    return y * g


def _seg_norm(y, bd, g):
    outs = []
    for c in range(y.shape[1] // SEG_CHUNK):
        yc = y[:, c * SEG_CHUNK:(c + 1) * SEG_CHUNK]
        ms = _dot3(yc * yc, bd)
        outs.append(yc * lax.rsqrt(ms + RMS_EPS))
    out = outs[0] if len(outs) == 1 else jnp.concatenate(outs, axis=1)
    return out * g


def _silu(x):
    return x * jax.nn.sigmoid(x)


def _cparams(sem):
    return pltpu.CompilerParams(dimension_semantics=sem, vmem_limit_bytes=VMEM_LIMIT)


def _seg_matrix():
    i = np.arange(SEG_CHUNK)
    m = (i[:, None] // HEAD_DIM == i[None, :] // HEAD_DIM).astype(np.float32) / HEAD_DIM
    return jnp.asarray(m, BF16)


def _proj_a_kernel(x_ref, g_ref, wq_ref, wkv_ref, wg_ref, bd_ref, qn_ref, kn1_ref, kn2_ref, gb_ref,
                   q_out, cmp_out, sel_out, win_out, gate_out, selb_out, winb_out):
    h = _rms(x_ref[...], g_ref[...]).astype(BF16)
    bd = bd_ref[...]
    q_out[...] = _seg_norm(_dot(h, wq_ref[...]), bd, qn_ref[...])
    kv = _dot(h, wkv_ref[...])
    kvw = 2 * N_KV_A * HEAD_DIM
    kw = N_KV_A * HEAD_DIM
    cmp_out[...] = kv[:, :kvw]
    for base, kn_ref, out, outb in ((kvw, kn1_ref, sel_out, selb_out), (2 * kvw, kn2_ref, win_out, winb_out)):
        kn = _seg_norm(kv[:, base:base + kw], bd, kn_ref[...])
        v = kv[:, base + kw:base + kvw]
        out[:, :kw] = kn
        out[:, kw:] = v
        outb[:, :kw] = kn.astype(BF16)
        outb[:, kw:] = v.astype(BF16)
    gate_out[...] = jax.nn.sigmoid(_dot(h, wg_ref[...]) + gb_ref[...])


def _proj_a(x, g, wq, wkv, wg, bd, qn, kn1, kn2, gb, tm):
    t, d = x.shape
    kvw = 2 * N_KV_A * HEAD_DIM
    const = lambda shape: pl.BlockSpec(shape, lambda i: (0, 0))
    row = lambda w: pl.BlockSpec((tm, w), lambda i: (i, 0))
    return pl.pallas_call(
        _proj_a_kernel,
        grid=(t // tm,),
        in_specs=[row(d), const((1, d)), const(wq.shape), const(wkv.shape), const(wg.shape),
                  const(bd.shape), const(qn.shape), const(kn1.shape), const(kn2.shape), const(gb.shape)],
        out_specs=[row(d), row(kvw), row(kvw), row(kvw), row(LANES), row(kvw), row(kvw)],
        out_shape=[jax.ShapeDtypeStruct((t, d), F32)] + [jax.ShapeDtypeStruct((t, kvw), F32)] * 3
                  + [jax.ShapeDtypeStruct((t, LANES), F32)] + [jax.ShapeDtypeStruct((t, kvw), BF16)] * 2,
        compiler_params=_cparams(("parallel",)),
        name="proj_a",
    )(x, g, wq, wkv, wg, bd, qn, kn1, kn2, gb)


def _proj_b_kernel(x_ref, g_ref, wq_ref, wk_ref, wv_ref, wf_ref, bd_ref, qn_ref, kn_ref, fb_ref,
                   q_out, kv_out, lf_out, kvb_out):
    h = _rms(x_ref[...], g_ref[...]).astype(BF16)
    bd = bd_ref[...]
    d = q_out.shape[1]
    q_out[...] = _seg_norm(_dot(h, wq_ref[...]), bd, qn_ref[...])
    kn = _seg_norm(_dot(h, wk_ref[...]), bd, kn_ref[...])
    v = _dot(h, wv_ref[...])
    kv_out[:, :d] = kn
    kv_out[:, d:] = v
    kvb_out[:, :d] = kn.astype(BF16)
    kvb_out[:, d:] = v.astype(BF16)
    z = _dot(h, wf_ref[...]) + fb_ref[...]
    lf_out[...] = jnp.minimum(z, 0.0) - jnp.log1p(jnp.exp(-jnp.abs(z)))


def _proj_b(x, g, wq, wk, wv, wf, bd, qn, kn, fb, tm):
    t, d = x.shape
    const = lambda shape: pl.BlockSpec(shape, lambda i: (0, 0))
    row = lambda w: pl.BlockSpec((tm, w), lambda i: (i, 0))
    return pl.pallas_call(
        _proj_b_kernel,
        grid=(t // tm,),
        in_specs=[row(d), const((1, d)), const(wq.shape), const(wk.shape), const(wv.shape), const(wf.shape),
                  const(bd.shape), const(qn.shape), const(kn.shape), const(fb.shape)],
        out_specs=[row(d), row(2 * d), row(LANES), row(2 * d)],
        out_shape=[jax.ShapeDtypeStruct((t, d), F32), jax.ShapeDtypeStruct((t, 2 * d), F32),
                   jax.ShapeDtypeStruct((t, LANES), F32), jax.ShapeDtypeStruct((t, 2 * d), BF16)],
        compiler_params=_cparams(("parallel",)),
        name="proj_b",
    )(x, g, wq, wk, wv, wf, bd, qn, kn, fb)


def _resid_proj_kernel(x_ref, a_ref, w_ref, o_ref):
    o_ref[...] = x_ref[...] + _dot(a_ref[...].astype(BF16), w_ref[...])


def _resid_proj(x, a, w, tm):
    t, d = x.shape
    row = pl.BlockSpec((tm, d), lambda i: (i, 0))
    return pl.pallas_call(
        _resid_proj_kernel,
        grid=(t // tm,),
        in_specs=[row, row, pl.BlockSpec(w.shape, lambda i: (0, 0))],
        out_specs=row,
        out_shape=jax.ShapeDtypeStruct((t, d), F32),
        compiler_params=_cparams(("parallel",)),
        name="resid_proj",
    )(x, a, w)


def _ffn_kernel(x_ref, g_ref, wg_ref, wu_ref, wd_ref, o_ref, xn_ref, acc_ref):
    j = pl.program_id(1)

    @pl.when(j == 0)
    def _():
        xn_ref[...] = _rms(x_ref[...], g_ref[...]).astype(BF16)
        acc_ref[...] = jnp.zeros_like(acc_ref)

    xn = xn_ref[...]
    hmid = _silu(_dot(xn, wg_ref[...])) * _dot(xn, wu_ref[...])
    acc_ref[...] += _dot(hmid.astype(BF16), wd_ref[...])

    @pl.when(j == pl.num_programs(1) - 1)
    def _():
        o_ref[...] = x_ref[...] + acc_ref[...]


def _ffn(x, g, wg, wu, wd, tm, tf):
    t, d = x.shape
    dff = wg.shape[1]
    row = pl.BlockSpec((tm, d), lambda i, j: (i, 0))
    return pl.pallas_call(
        _ffn_kernel,
        grid=(t // tm, dff // tf),
        in_specs=[row, pl.BlockSpec((1, d), lambda i, j: (0, 0)),
                  pl.BlockSpec((d, tf), lambda i, j: (0, j)),
                  pl.BlockSpec((d, tf), lambda i, j: (0, j)),
                  pl.BlockSpec((tf, d), lambda i, j: (j, 0))],
        out_specs=row,
        out_shape=jax.ShapeDtypeStruct((t, d), F32),
        scratch_shapes=[pltpu.VMEM((tm, d), BF16), pltpu.VMEM((tm, d), F32)],
        compiler_params=_cparams(("parallel", "arbitrary")),
        name="ffn",
    )(x, g, wg, wu, wd)


def _moe_kernel(x_ref, g_ref, r_ref, wg_ref, wu_ref, wd_ref, o_ref, xn_ref, gate_ref, acc_ref, eacc_ref):
    e = pl.program_id(1)
    j = pl.program_id(2)
    last_j = pl.num_programs(2) - 1

    @pl.when((e == 0) & (j == 0))
    def _():
        xn = _rms(x_ref[...], g_ref[...]).astype(BF16)
        xn_ref[...] = xn
        acc_ref[...] = jnp.zeros_like(acc_ref)
        logits = _dot(xn, r_ref[...])
        col = lax.broadcasted_iota(jnp.int32, logits.shape, 1).astype(F32)
        real = col < N_EXPERTS
        logits = jnp.where(real, logits, NEG)
        mx = jnp.max(logits, axis=-1, keepdims=True)
        pe = jnp.where(real, jnp.exp(logits - mx), 0.0)
        probs = pe / jnp.sum(pe, axis=-1, keepdims=True)
        work = jnp.where(real, probs, PICKED)
        picked = jnp.zeros(logits.shape, jnp.bool_)
        for _ in range(2):
            top = jnp.max(work, axis=-1, keepdims=True)
            idx = jnp.min(jnp.where(work == top, col, float(LANES)), axis=-1, keepdims=True)
            hit = col == idx
            picked = picked | hit
            work = jnp.where(hit, PICKED, work)
        sel = jnp.where(picked, probs, 0.0)
        gate_ref[...] = sel / jnp.sum(sel, axis=-1, keepdims=True)

    @pl.when(j == 0)
    def _():
        eacc_ref[...] = jnp.zeros_like(eacc_ref)

    xn = xn_ref[...]
    hmid = _silu(_dot(xn, wg_ref[...])) * _dot(xn, wu_ref[...])
    eacc_ref[...] += _dot(hmid.astype(BF16), wd_ref[...])

    @pl.when(j == last_j)
    def _():
        gate = gate_ref[...]
        col = lax.broadcasted_iota(jnp.int32, gate.shape, 1)
        gcol = jnp.sum(jnp.where(col == e, gate, 0.0), axis=-1, keepdims=True)
        acc_ref[...] += gcol * eacc_ref[...]

    @pl.when((e == pl.num_programs(1) - 1) & (j == last_j))
    def _():
        o_ref[...] = x_ref[...] + acc_ref[...]


def _moe(x, g, router, wg, wu, wd, tm, tf):
    t, d = x.shape
    ne, _, dff = wg.shape
    row = pl.BlockSpec((tm, d), lambda i, e, j: (i, 0))
    return pl.pallas_call(
        _moe_kernel,
        grid=(t // tm, ne, dff // tf),
        in_specs=[row, pl.BlockSpec((1, d), lambda i, e, j: (0, 0)),
                  pl.BlockSpec(router.shape, lambda i, e, j: (0, 0)),
                  pl.BlockSpec((None, d, tf), lambda i, e, j: (e, 0, j)),
                  pl.BlockSpec((None, d, tf), lambda i, e, j: (e, 0, j)),
                  pl.BlockSpec((None, tf, d), lambda i, e, j: (e, j, 0))],
        out_specs=row,
        out_shape=jax.ShapeDtypeStruct((t, d), F32),
        scratch_shapes=[pltpu.VMEM((tm, d), BF16), pltpu.VMEM((tm, LANES), F32),
                        pltpu.VMEM((tm, d), F32), pltpu.VMEM((tm, d), F32)],
        compiler_params=_cparams(("parallel", "arbitrary", "arbitrary")),
        name="moe",
    )(x, g, router, wg, wu, wd)


ROWS_PER_PAGE = PAGE // CMP_STRIDE


def _compress_kernel(pt_ref, x_hbm, pos_ref, w1_ref, w2_ref, kn_ref, o_ref, xbuf, sem, *, n_pages, n_kvg):
    b = pl.program_id(0)
    c = pl.program_id(1)
    step = b * n_kvg + c
    n_steps = pl.num_programs(0) * n_kvg
    slot = step % 2

    def page_copy(bb, cc, sl, p):
        return pltpu.make_async_copy(
            x_hbm.at[pt_ref[bb, p], cc],
            xbuf.at[sl, pl.ds(p * ROWS_PER_PAGE, ROWS_PER_PAGE), :],
            sem.at[sl])

    def fetch(bb, cc, sl):
        def body(p, carry):
            page_copy(bb, cc, sl, p).start()
            return carry
        lax.fori_loop(0, n_pages, body, 0)

    @pl.when(step == 0)
    def _():
        fetch(b, c, slot)

    @pl.when(step + 1 < n_steps)
    def _():
        nxt = step + 1
        fetch(nxt // n_kvg, nxt % n_kvg, 1 - slot)

    def wait_body(p, carry):
        page_copy(b, c, slot, p).wait()
        return carry
    lax.fori_loop(0, n_pages, wait_body, 0)

    x = xbuf[slot]
    n_rows = x.shape[0]
    z0 = _dot((x + pos_ref[0]).astype(BF16), w1_ref[0])
    z1 = _dot((x + pos_ref[1]).astype(BF16), w1_ref[1])
    hid = z0 + pltpu.roll(z1, n_rows - 1, 0)
    y = _dot(_silu(hid).astype(BF16), w2_ref[...])
    yn = _rms(y, kn_ref[...])
    is_key = c < (n_kvg // 2)
    o_ref[...] = jnp.where(is_key, yn, y)


def _compress(pt, xr, posr, w1r, w2, kn):
    nb, n_pages = pt.shape
    n_kvg = xr.shape[1]
    half = n_kvg // 2
    n_rows = n_pages * ROWS_PER_PAGE
    width = xr.shape[3]
    grid_spec = pltpu.PrefetchScalarGridSpec(
        num_scalar_prefetch=1,
        grid=(nb, n_kvg),
        in_specs=[pl.BlockSpec(memory_space=pl.ANY),
                  pl.BlockSpec((None, 2, 1, width), lambda b, c, pt: (c // half, 0, 0, 0)),
                  pl.BlockSpec((None, 2, width, CMP_HIDDEN), lambda b, c, pt: (c // half, 0, 0, 0)),
                  pl.BlockSpec((None, CMP_HIDDEN, HEAD_DIM), lambda b, c, pt: (c // half, 0, 0)),
                  pl.BlockSpec((1, HEAD_DIM), lambda b, c, pt: (0, 0))],
        out_specs=pl.BlockSpec((None, None, n_rows, HEAD_DIM), lambda b, c, pt: (b, c, 0, 0)),
        scratch_shapes=[pltpu.VMEM((2, n_rows, width), F32), pltpu.SemaphoreType.DMA((2,))],
    )
    return pl.pallas_call(
        functools.partial(_compress_kernel, n_pages=n_pages, n_kvg=n_kvg),
        grid_spec=grid_spec,
        out_shape=jax.ShapeDtypeStruct((nb, n_kvg, n_rows, HEAD_DIM), F32),
        compiler_params=_cparams(("arbitrary", "arbitrary")),
        name="compress",
    )(pt, xr, posr, w1r, w2, kn)


def _cmp_topk_kernel(q_ref, pos_ref, kc_ref, vc_ref, ov_ref, gate_ref, o_ref, sb_ref, *,
                     tq, n_cmp, n_sel, n_top, slopes2):
    nc = kc_ref.shape[0]
    ns = ov_ref.shape[1]
    q = q_ref[...]
    pos = pos_ref[...]
    gates = gate_ref[...]
    n_idx = lax.broadcasted_iota(jnp.int32, (1, nc), 1)
    dcmp = pos - (n_idx * CMP_STRIDE + (CMP_BLOCK - 1))
    mask = (dcmp >= 0) & (n_idx < n_cmp)
    rel = (n_idx * CMP_STRIDE + (CMP_BLOCK - 1) - pos[0:1, :]).astype(F32)
    blk = lax.broadcasted_iota(jnp.int32, (1, ns), 1)
    blkf = blk.astype(F32)
    cur = pos // SEL_BLOCK
    forced = (blk == 0) | (blk == cur) | (blk == cur - 1)
    valid = (blk * SEL_BLOCK <= pos) & (blk < n_sel)
    ov = ov_ref[...]
    for g in range(N_KV_A):
        heads = range(g * GROUP_A, (g + 1) * GROUP_A)
        qg = (jnp.concatenate([q[:, h * HEAD_DIM:(h + 1) * HEAD_DIM] for h in heads], axis=0)
              * (ATTN_SCALE * LOG2E)).astype(BF16)
        kg = kc_ref[:, g * HEAD_DIM:(g + 1) * HEAD_DIM].astype(BF16)
        vg = vc_ref[:, g * HEAD_DIM:(g + 1) * HEAD_DIM].astype(BF16)
        s = _dot_nt(qg, kg)
        psum = jnp.zeros((tq, nc), F32)
        for r, h in enumerate(heads):
            sr = jnp.where(mask, s[r * tq:(r + 1) * tq] + slopes2[h] * rel, NEG)
            mx = jnp.max(sr, axis=-1, keepdims=True)
            p = jnp.where(mask, jnp.exp2(sr - mx), 0.0)
            p = p * (1.0 / jnp.maximum(jnp.sum(p, axis=-1, keepdims=True), 1e-30))
            o = _dot(p.astype(BF16), vg)
            o_ref[:, h * HEAD_DIM:(h + 1) * HEAD_DIM] = gates[:, h:h + 1] * o
            psum = psum + p
        imp = _dot(psum.astype(BF16), ov)
        imp = jnp.where(forced, FORCE_SCORE, imp)
        work = jnp.where(valid, imp, NEG)
        work = jnp.where(blk < n_sel, work, PICKED)
        picked = jnp.zeros((tq, ns), jnp.bool_)
        for _ in range(n_top):
            top = jnp.max(work, axis=-1, keepdims=True)
            idx = jnp.min(jnp.where(work == top, blkf, float(ns)), axis=-1, keepdims=True)
            hit = blkf == idx
            picked = picked | hit
            work = jnp.where(hit, PICKED, work)
        sb_ref[:, g * ns:(g + 1) * ns] = jnp.where(picked, 0.0, NEG).astype(BF16)


def _cmp_topk(q, pos, kc, vc, ov, gates, tq, tiles_per_seq, n_cmp, n_sel):
    t, d = q.shape
    nc = kc.shape[1]
    ns = ov.shape[1]
    kvw = N_KV_A * HEAD_DIM
    row = lambda w: pl.BlockSpec((tq, w), lambda i: (i, 0))
    seq = pl.BlockSpec((None, nc, kvw), lambda i: (i // tiles_per_seq, 0, 0))
    return pl.pallas_call(
        functools.partial(_cmp_topk_kernel, tq=tq, n_cmp=n_cmp, n_sel=n_sel,
                          n_top=min(TOP_N, n_sel), slopes2=_slopes_log2()),
        grid=(t // tq,),
        in_specs=[row(d), row(1), seq, seq, pl.BlockSpec(ov.shape, lambda i: (0, 0)), row(LANES)],
        out_specs=[row(d), row(N_KV_A * ns)],
        out_shape=[jax.ShapeDtypeStruct((t, d), F32), jax.ShapeDtypeStruct((t, N_KV_A * ns), BF16)],
        compiler_params=_cparams(("parallel",)),
        name="cmp_topk",
    )(q, pos, kc, vc, ov, gates)


def _flash_kernel(*refs, kind, kv_heads, tq, tk, n_kk, slopes2, gate_off):
    qs_ref, m_ref, l_ref, acc_ref, p_ref = refs[-5:]
    if kind == "fox":
        q_ref, kv_ref, ct_ref, o_ref = refs[:-5]
    elif kind == "sel":
        q_ref, kv_ref, sb_ref, prev_ref, gate_ref, o_ref = refs[:-5]
    else:
        q_ref, kv_ref, prev_ref, gate_ref, o_ref = refs[:-5]
    rep = N_HEADS // kv_heads
    v_off = kv_heads * HEAD_DIM
    i = pl.program_id(0)
    kk = pl.program_id(1)
    q0 = i * tq
    diag = (q0 + tq - 1) // tk
    if kind == "win":
        kt = diag - (n_kk - 1) + kk
        active = kt >= 0
        last = kk == n_kk - 1
    else:
        kt = kk
        active = kk <= diag
        last = kk == diag
    low = lax.broadcasted_iota(jnp.int32, (1, LANES), 1) < HEAD_DIM

    def kv_half(h):
        return (h // rep) % 2

    @pl.when(kk == 0)
    def _():
        m_ref[...] = jnp.full(m_ref.shape, NEG, F32)
        l_ref[...] = jnp.zeros_like(l_ref)
        acc_ref[...] = jnp.zeros_like(acc_ref)
        for j in range(N_HEADS // 2):
            q2 = q_ref[:, j * LANES:(j + 1) * LANES] * (ATTN_SCALE * LOG2E)
            for e in range(2):
                h = 2 * j + e
                mine = jnp.where(low if e == 0 else jnp.logical_not(low), q2, 0.0)
                if kv_half(h) != e:
                    mine = pltpu.roll(mine, HEAD_DIM, 1)
                qs_ref[h * tq:(h + 1) * tq, :] = mine.astype(BF16)

    def step(masked):
        kpos = kt * tk + lax.broadcasted_iota(jnp.int32, (1, tk), 1)
        if kind != "fox":
            rel = (kpos - q0).astype(F32)
        if kind == "sel":
            sbt = sb_ref[...]
            srow = lax.broadcasted_iota(jnp.int32, (sbt.shape[1], 1), 0)
            blk_in_tile = lax.broadcasted_iota(jnp.int32, (1, tk), 1) // SEL_BLOCK
            per_tile = tk // SEL_BLOCK
        for g in range(kv_heads):
            pair = g // 2
            k2 = kv_ref[:, pair * LANES:(pair + 1) * LANES]
            v2 = kv_ref[:, v_off + pair * LANES:v_off + (pair + 1) * LANES]
            if kind == "fox":
                brow = -ct_ref[g:g + 1, :]
            if kind == "sel":
                selb = _dot(sbt, (srow == g * per_tile + blk_in_tile).astype(BF16))
            for r in range(rep):
                h = g * rep + r
                hr = slice(h * tq, (h + 1) * tq)
                s = _dot_nt(qs_ref[hr, :], k2)
                if kind != "fox":
                    brow = slopes2[h] * rel
                for c in range(tq // ROW_CHUNK):
                    cr = slice(c * ROW_CHUNK, (c + 1) * ROW_CHUNK)
                    hc = slice(h * tq + c * ROW_CHUNK, h * tq + (c + 1) * ROW_CHUNK)
                    sc = s[cr] + brow
                    if kind == "sel":
                        sc = sc + selb[cr]
                    if masked:
                        tpos = q0 + c * ROW_CHUNK + lax.broadcasted_iota(jnp.int32, (ROW_CHUNK, 1), 0)
                        dist = tpos - kpos
                        keep = (dist >= 0) & (dist < WINDOW) if kind == "win" else dist >= 0
                        sc = jnp.where(keep, sc, NEG)
                    m_prev = m_ref[hc, :]
                    m_new = jnp.maximum(m_prev, jnp.max(sc, axis=-1, keepdims=True))
                    alpha = jnp.exp2(m_prev - m_new)
                    pc = jnp.exp2(sc - jnp.tile(m_new, (1, tk // LANES)))
                    l_ref[hc, :] = alpha * l_ref[hc, :] + jnp.sum(pc, axis=-1, keepdims=True)
                    acc_ref[hc, :] = alpha * acc_ref[hc, :]
                    m_ref[hc, :] = m_new
                    p_ref[h % 2, cr, :] = pc.astype(BF16)
                acc_ref[hr, :] += _dot(p_ref[h % 2], v2)

    if kind == "win":
        pl.when(active)(lambda: step(True))
    else:
        crosses = (kt + 1) * tk - 1 > q0
        pl.when(active & crosses)(lambda: step(True))
        pl.when(active & jnp.logical_not(crosses))(lambda: step(False))

    @pl.when(last)
    def _():
        for j in range(N_HEADS // 2):
            halves = []
            for e in range(2):
                h = 2 * j + e
                hr = slice(h * tq, (h + 1) * tq)
                o = acc_ref[hr, :] / l_ref[hr, :]
                if kv_half(h) != e:
                    o = pltpu.roll(o, HEAD_DIM, 1)
                halves.append(o)
            o2 = jnp.where(low, halves[0], halves[1])
            cols = slice(j * LANES, (j + 1) * LANES)
            if kind == "fox":
                o_ref[:, cols] = o2
            else:
                c0 = gate_off + 2 * j
                g2 = jnp.where(low, gate_ref[:, c0:c0 + 1], gate_ref[:, c0 + 1:c0 + 2])
                o_ref[:, cols] = prev_ref[:, cols] + g2 * o2


def _flash(kind, q, kv, extras, tq, tk):
    t, d = q.shape
    kv_heads = kv.shape[1] // (2 * HEAD_DIM)
    nq = t // tq
    if kind == "win":
        q0 = np.arange(nq) * tq
        n_kk = int(np.max((q0 + tq - 1) // tk - np.maximum(q0 - (WINDOW - 1), 0) // tk + 1))
    else:
        n_kk = t // tk

    def kt_of(i, kk):
        diag = ((i + 1) * tq - 1) // tk
        if kind == "win":
            return jnp.maximum(diag - (n_kk - 1) + kk, 0)
        return jnp.minimum(kk, diag)

    qrow = lambda w: pl.BlockSpec((tq, w), lambda i, kk: (i, 0))
    in_specs = [qrow(d), pl.BlockSpec((tk, kv.shape[1]), lambda i, kk: (kt_of(i, kk), 0))]
    if kind == "fox":
        (ct2,) = extras
        in_specs += [pl.BlockSpec((ct2.shape[0], tk), lambda i, kk: (0, kt_of(i, kk)))]
        gate_off = 0
    elif kind == "sel":
        sbt, prev, gates = extras
        in_specs += [pl.BlockSpec((None, tq, sbt.shape[2]), lambda i, kk: (kt_of(i, kk), i, 0)),
                     qrow(d), qrow(LANES)]
        gate_off = N_HEADS
    else:
        prev, gates = extras
        in_specs += [qrow(d), qrow(LANES)]
        gate_off = 2 * N_HEADS
    return pl.pallas_call(
        functools.partial(_flash_kernel, kind=kind, kv_heads=kv_heads, tq=tq, tk=tk, n_kk=n_kk,
                          slopes2=_slopes_log2(), gate_off=gate_off),
        grid=(nq, n_kk),
        in_specs=in_specs,
        out_specs=qrow(d),
        out_shape=jax.ShapeDtypeStruct((t, d), F32),
        scratch_shapes=[pltpu.VMEM((N_HEADS * tq, LANES), BF16), pltpu.VMEM((N_HEADS * tq, LANES), F32),
                        pltpu.VMEM((N_HEADS * tq, LANES), F32), pltpu.VMEM((N_HEADS * tq, LANES), F32),
                        pltpu.VMEM((2, tq, tk), BF16)],
        compiler_params=_cparams(("parallel", "arbitrary")),
        name="flash_" + kind,
    )(q, kv, *extras)


def _cumsum_kernel(x_ref, o_ref, carry_ref):
    @pl.when(pl.program_id(0) == 0)
    def _():
        carry_ref[...] = jnp.zeros_like(carry_ref)

    x = x_ref[...]
    tm = x.shape[1]
    upper = (lax.broadcasted_iota(jnp.int32, (tm, tm), 0)
             <= lax.broadcasted_iota(jnp.int32, (tm, tm), 1)).astype(BF16)
    c = _dot3(x, upper) + carry_ref[:, 0:1]
    o_ref[...] = c * LOG2E
    carry_ref[...] = jnp.broadcast_to(c[:, tm - 1:tm], carry_ref.shape)


def _cumsum_t(xt, tm):
    nh, t = xt.shape
    blk = pl.BlockSpec((nh, tm), lambda i: (0, i))
    return pl.pallas_call(
        _cumsum_kernel,
        grid=(t // tm,),
        in_specs=[blk],
        out_specs=blk,
        out_shape=jax.ShapeDtypeStruct((nh, t), F32),
        scratch_shapes=[pltpu.VMEM((nh, LANES), F32)],
        compiler_params=_cparams(("arbitrary",)),
        name="cumsum",
    )(xt)


def _decode_kernel(*refs, kind, n_pp, n_pages, n_new, past, key_base):
    pt_ref = refs[0]
    qbd_ref, slope_ref, scale_ref, prev_ref, knew_ref, vnew_ref = refs[1:7]
    k = 7
    if kind == "fox":
        lfn_ref = refs[k]
        k += 1
    elif kind == "sel":
        sb_ref = refs[k]
        k += 1
    page_refs = refs[k:k + n_pp]
    k += n_pp
    if kind == "fox":
        lf_refs = refs[k:k + n_pp]
        k += n_pp
    o_ref = refs[k]
    m_ref, l_ref, acc_ref = refs[k + 1:k + 4]
    if kind == "fox":
        carry_ref = refs[k + 4]
    del pt_ref

    s_id = pl.program_id(1)
    n_rows, c_dim = qbd_ref.shape
    kv_heads = c_dim // HEAD_DIM
    row = lax.broadcasted_iota(jnp.int32, (n_rows, 1), 0)
    t_row = row // N_HEADS
    h_row = row % N_HEADS
    lane = lax.broadcasted_iota(jnp.int32, (1, PAGE), 1)
    qbd = qbd_ref[...]

    @pl.when(s_id == 0)
    def _():
        m_ref[...] = jnp.full(m_ref.shape, NEG, F32)
        l_ref[...] = jnp.zeros_like(l_ref)
        acc_ref[...] = jnp.zeros_like(acc_ref)
        if kind == "fox":
            carry_ref[...] = jnp.zeros_like(carry_ref)

    if kind == "fox":
        incl = (lax.broadcasted_iota(jnp.int32, (PAGE, PAGE), 0)
                <= lax.broadcasted_iota(jnp.int32, (PAGE, PAGE), 1)).astype(BF16)
        cnu = _dot3(lfn_ref[...], incl)
        cn = jnp.sum(jnp.where(lane == t_row, cnu, 0.0), axis=-1, keepdims=True)
        after = (lax.broadcasted_iota(jnp.int32, (PAGE, PAGE), 0)
                 > lax.broadcasted_iota(jnp.int32, (PAGE, PAGE), 1)).astype(BF16)
    else:
        slope = slope_ref[...]
        qpos = past + t_row

    def update(s, v_parts):
        m_prev = m_ref[...]
        m_new = jnp.maximum(m_prev, jnp.max(s, axis=-1, keepdims=True))
        alpha = jnp.exp(m_prev - m_new)
        p = jnp.exp(s - m_new)
        l_ref[...] = alpha * l_ref[...] + jnp.sum(p, axis=-1, keepdims=True)
        acc = alpha * acc_ref[...]
        for j, (v, transposed) in enumerate(v_parts):
            pj = p[:, j * PAGE:(j + 1) * PAGE].astype(BF16)
            acc = acc + (_dot_nt(pj, v) if transposed else _dot(pj, v))
        acc_ref[...] = acc
        m_ref[...] = m_new

    s_parts, b_parts, v_parts = [], [], []
    for j in range(n_pp):
        page_idx = n_pages - 1 - (s_id * n_pp + j)
        kt = page_refs[j][0].astype(BF16)
        s_parts.append(_dot(qbd, kt))
        v_parts.append((page_refs[j][1].astype(BF16), True))
        if kind == "fox":
            lf = lf_refs[j][...]
            carry = carry_ref[:, 0:1]
            suf = _dot3(lf, after) + carry
            carry_ref[...] = jnp.broadcast_to(carry + jnp.sum(lf, axis=-1, keepdims=True), carry_ref.shape)
            b_parts.append(jnp.concatenate([suf] * (n_rows // N_HEADS), axis=0) + cn)
        else:
            kbuf = page_idx * PAGE + lane
            bias = -slope * (qpos - (key_base + kbuf)).astype(F32)
            if kind == "win":
                bias = jnp.where(kbuf > t_row, bias, NEG)
            b_parts.append(bias)
    s = jnp.concatenate(s_parts, axis=1) + jnp.concatenate(b_parts, axis=1)
    if kind == "sel":
        sb = sb_ref[...]
        ns = sb.shape[1]
        kpos = jnp.concatenate(
            [(n_pages - 1 - (s_id * n_pp + j)) * PAGE + lane for j in range(n_pp)], axis=1)
        expand = (lax.broadcasted_iota(jnp.int32, (ns, 1), 0) == kpos // SEL_BLOCK).astype(BF16)
        s = s + _dot(sb, expand)
    update(s, v_parts)

    @pl.when(s_id == pl.num_programs(1) - 1)
    def _():
        s_new = _dot_nt(qbd, knew_ref[...].astype(BF16))
        ok = (lane <= t_row) & (lane < n_new)
        if kind == "fox":
            bias = cn - cnu
        else:
            bias = -slope * (t_row - lane).astype(F32)
            if kind == "sel":
                ns = sb_ref.shape[1]
                expand_new = (lax.broadcasted_iota(jnp.int32, (ns, 1), 0)
                              == (past + lane) // SEL_BLOCK).astype(BF16)
                bias = bias + _dot(sb_ref[...], expand_new)
        update(jnp.where(ok, s_new + bias, NEG), [(vnew_ref[...].astype(BF16), False)])
        o_full = acc_ref[...] / l_ref[...]
        col_head = lax.broadcasted_iota(jnp.int32, (1, c_dim), 1) // HEAD_DIM
        own = col_head == h_row // (N_HEADS // kv_heads)
        o_full = jnp.where(own, o_full, 0.0)
        o = o_full[:, 0:HEAD_DIM]
        for c in range(1, kv_heads):
            o = o + o_full[:, c * HEAD_DIM:(c + 1) * HEAD_DIM]
        o_ref[...] = prev_ref[...] + scale_ref[...] * o


def _decode(kind, pt, qbd, slope_rows, scale_rows, prev, knew, vnew, extra, pages, lf_pages,
            n_pp, n_new, past, key_base):
    nb, n_pages = pt.shape
    n_rows, c_dim = qbd.shape[1:]
    n_steps = n_pages // n_pp
    seq = lambda shape: pl.BlockSpec((None,) + shape, lambda b, s, pt: (b,) + (0,) * len(shape))

    def page_map(j):
        return lambda b, s, pt: (pt[b, n_pages - 1 - (s * n_pp + j)], 0, 0, 0)

    def lf_map(j):
        return lambda b, s, pt: (pt[b, n_pages - 1 - (s * n_pp + j)], 0, 0)

    in_specs = [seq((n_rows, c_dim)), pl.BlockSpec((n_rows, 1), lambda b, s, pt: (0, 0)),
                seq((n_rows, 1)), seq((n_rows, HEAD_DIM)), seq((PAGE, c_dim)), seq((PAGE, c_dim))]
    args = [qbd, slope_rows, scale_rows, prev, knew, vnew]
    if extra is not None:
        in_specs.append(seq(extra.shape[1:]))
        args.append(extra)
    in_specs += [pl.BlockSpec((None, 2, c_dim, PAGE), page_map(j)) for j in range(n_pp)]
    args += [pages] * n_pp
    scratch = [pltpu.VMEM((n_rows, 1), F32), pltpu.VMEM((n_rows, 1), F32), pltpu.VMEM((n_rows, c_dim), F32)]
    if kind == "fox":
        in_specs += [pl.BlockSpec((None, N_HEADS, PAGE), lf_map(j)) for j in range(n_pp)]
        args += [lf_pages] * n_pp
        scratch.append(pltpu.VMEM((N_HEADS, LANES), F32))
    grid_spec = pltpu.PrefetchScalarGridSpec(
        num_scalar_prefetch=1, grid=(nb, n_steps), in_specs=in_specs,
        out_specs=seq((n_rows, HEAD_DIM)), scratch_shapes=scratch)
    return pl.pallas_call(
        functools.partial(_decode_kernel, kind=kind, n_pp=n_pp, n_pages=n_pages, n_new=n_new,
                          past=past, key_base=key_base),
        grid_spec=grid_spec,
        out_shape=jax.ShapeDtypeStruct((nb, n_rows, HEAD_DIM), F32),
        compiler_params=_cparams(("parallel", "arbitrary")),
        name="decode_" + kind,
    )(pt, *args)


def _overlap_matrix(n_cmp_pad, n_cmp, n_sel_pad, n_sel):
    cs = np.arange(n_cmp_pad)[:, None] * CMP_STRIDE
    ss = np.arange(n_sel_pad)[None, :] * SEL_BLOCK
    ov = np.minimum(cs + CMP_BLOCK, ss + SEL_BLOCK) - np.maximum(cs, ss)
    ov = np.clip(ov, 0, None) / CMP_BLOCK
    ov = ov * (np.arange(n_cmp_pad)[:, None] < n_cmp) * (np.arange(n_sel_pad)[None, :] < n_sel)
    return jnp.asarray(ov.astype(np.float32), BF16)


def _tile_heads(v, n):
    return jnp.tile(v.astype(F32), n)[None, :]


def _pad_cols(w, width):
    return jnp.pad(w, ((0, 0), (0, width - w.shape[1])))


def _block_diag_q(q, kv_heads):
    nb, t = q.shape[:2]
    own = (np.arange(N_HEADS)[:, None] // (N_HEADS // kv_heads) == np.arange(kv_heads)[None, :])
    qb = q[:, :, :, None, :] * jnp.asarray(own, F32)[None, None, :, :, None]
    return (qb * ATTN_SCALE).reshape(nb, t * N_HEADS, kv_heads * HEAD_DIM).astype(BF16)


def _pages_t(cache):
    pool, rows, two, heads, hd = cache.shape
    return jnp.transpose(cache, (0, 2, 3, 4, 1)).reshape(pool, two, heads * hd, rows)


def _pad_rows(x, rows):
    return jnp.pad(x, ((0, 0), (0, rows - x.shape[1]), (0, 0)))


def _nsa_weights(a_w_in, a_q_norm, a_k_norm, a_gate_b, a_cmp_pos, a_cmp_w1, a_cmp_w2, d):
    qc = N_HEADS * HEAD_DIM
    kvw = 2 * N_KV_A * HEAD_DIM
    w = a_w_in.astype(BF16)
    half = CMP_STRIDE * HEAD_DIM
    return dict(
        wq=w[:, :qc], wkv=w[:, qc:qc + 3 * kvw], wg=_pad_cols(w[:, qc + 3 * kvw:], LANES),
        qn=_tile_heads(a_q_norm, N_HEADS), kn0=a_k_norm[0][None, :].astype(F32),
        kn1=_tile_heads(a_k_norm[1], N_KV_A), kn2=_tile_heads(a_k_norm[2], N_KV_A),
        gb=_pad_cols(a_gate_b[None, :].astype(F32), LANES),
        posr=a_cmp_pos.reshape(2, 2, 1, half).astype(F32),
        w1r=a_cmp_w1.reshape(2, 2, half, CMP_HIDDEN).astype(BF16),
        w2=a_cmp_w2.astype(BF16))


def _row_groups(x, n_kvg):
    rows = x.shape[0]
    r = rows // CMP_STRIDE
    xr = x.reshape(r // ROWS_PER_PAGE, ROWS_PER_PAGE, CMP_STRIDE, n_kvg, HEAD_DIM)
    return jnp.transpose(xr, (0, 3, 1, 2, 4)).reshape(r // ROWS_PER_PAGE, n_kvg, ROWS_PER_PAGE,
                                                      CMP_STRIDE * HEAD_DIM)


def _split_kc_vc(cmp_out):
    nb, _, rows, _ = cmp_out.shape
    both = jnp.transpose(cmp_out.reshape(nb, 2, N_KV_A, rows, HEAD_DIM), (0, 1, 3, 2, 4))
    both = both.reshape(nb, 2, rows, N_KV_A * HEAD_DIM)
    return both[:, 0], both[:, 1]


def _nsa_prompt(xp, g_mix, wts, tm, tq_cmp, tq, tk):
    t, d = xp.shape
    bd = _seg_matrix()
    q, kv_cmp, kv_sel, kv_win, gates, sel_b, win_b = _proj_a(
        xp, g_mix, wts["wq"], wts["wkv"], wts["wg"], bd, wts["qn"], wts["kn1"], wts["kn2"], wts["gb"], tm)
    n_kvg = 2 * N_KV_A
    xr = _row_groups(kv_cmp, n_kvg)
    pt = jnp.arange(xr.shape[0], dtype=jnp.int32)[None, :]
    kc, vc = _split_kc_vc(_compress(pt, xr, wts["posr"], wts["w1r"], wts["w2"], wts["kn0"]))
    n_cmp = (t - CMP_BLOCK) // CMP_STRIDE + 1
    n_sel = -(-t // SEL_BLOCK)
    ns_pad = -(-n_sel // LANES) * LANES
    ov = _overlap_matrix(kc.shape[1], n_cmp, ns_pad, n_sel)
    pos = jnp.arange(t, dtype=jnp.int32)[:, None]
    o_cmp, sb = _cmp_topk(q, pos, kc, vc, ov, gates, tq_cmp, t // tq_cmp, n_cmp, n_sel)
    per_tile = tk // SEL_BLOCK
    sbt = sb.reshape(t, N_KV_A, ns_pad)[:, :, :(t // tk) * per_tile].reshape(t, N_KV_A, t // tk, per_tile)
    sbt = jnp.transpose(sbt, (2, 0, 1, 3)).reshape(t // tk, t, N_KV_A * per_tile)
    o = _flash("sel", q, sel_b, (sbt, o_cmp, gates), tq, tk)
    o = _flash("win", q, win_b, (o, gates), tq, tk)
    return o, kv_cmp, kv_sel, kv_win


def _nsa_sample(xs, g_mix, wts, cache_cmp, cache_sel, cache_win, page_table, tm):
    nb, n_new, d = xs.shape
    n_pages = page_table.shape[1]
    past = n_pages * PAGE
    bd = _seg_matrix()
    q, kv_cmp, kv_sel, kv_win, gates, _, _ = _proj_a(xs.reshape(nb * n_new, d), g_mix, wts["wq"], wts["wkv"],
                                                     wts["wg"], bd, wts["qn"], wts["kn1"], wts["kn2"],
                                                     wts["gb"], tm)
    n_kvg = 2 * N_KV_A
    kvw = n_kvg * HEAD_DIM
    kw = N_KV_A * HEAD_DIM
    pool = cache_cmp.shape[0]
    xr = _row_groups(cache_cmp.reshape(pool * PAGE, kvw), n_kvg)
    kc, vc = _split_kc_vc(_compress(page_table, xr, wts["posr"], wts["w1r"], wts["w2"], wts["kn0"]))
    total = past + n_new
    n_cmp = (total - CMP_BLOCK) // CMP_STRIDE + 1
    assert n_cmp <= kc.shape[1] and (n_cmp - 1) * CMP_STRIDE + CMP_BLOCK <= past
    n_sel = -(-total // SEL_BLOCK)
    ns_pad = -(-n_sel // LANES) * LANES
    ov = _overlap_matrix(kc.shape[1], n_cmp, ns_pad, n_sel)
    rows_pad = 8
    qp = _pad_rows(q.reshape(nb, n_new, d), rows_pad).reshape(nb * rows_pad, d)
    gp = _pad_rows(gates.reshape(nb, n_new, LANES), rows_pad).reshape(nb * rows_pad, LANES)
    pos = jnp.tile(past + jnp.minimum(jnp.arange(rows_pad, dtype=jnp.int32), n_new - 1), nb)[:, None]
    o_cmp, sb = _cmp_topk(qp, pos, kc, vc, ov, gp, rows_pad, 1, n_cmp, n_sel)
    n_rows = n_new * N_HEADS
    o_cmp = o_cmp.reshape(nb, rows_pad, d)[:, :n_new].reshape(nb, n_rows, HEAD_DIM)
    sb = sb.reshape(nb, rows_pad, N_KV_A, ns_pad)[:, :n_new]
    sb = jnp.repeat(sb, GROUP_A, axis=2).reshape(nb, n_rows, ns_pad)
    qbd = _block_diag_q(q.reshape(nb, n_new, N_HEADS, HEAD_DIM), N_KV_A)
    slope_rows = jnp.asarray(np.tile(_alibi_slopes(), n_new)[:, None])
    g3 = gates.reshape(nb, n_new, LANES)
    scale_sel = g3[:, :, N_HEADS:2 * N_HEADS].reshape(nb, n_rows, 1)
    scale_win = g3[:, :, 2 * N_HEADS:3 * N_HEADS].reshape(nb, n_rows, 1)
    new_sel = kv_sel.reshape(nb, n_new, kvw)
    o = _decode("sel", page_table, qbd, slope_rows, scale_sel, o_cmp,
                _pad_rows(new_sel[:, :, :kw], PAGE), _pad_rows(new_sel[:, :, kw:], PAGE), sb,
                _pages_t(cache_sel), None, n_pp=min(16, n_pages), n_new=n_new, past=past, key_base=0)
    wb = cache_win.shape[1]
    assert wb == WINDOW
    win_pages = wb // PAGE
    win_t = _pages_t(cache_win.reshape(nb * win_pages, PAGE, 2, N_KV_A, HEAD_DIM))
    win_pt = jnp.arange(nb * win_pages, dtype=jnp.int32).reshape(nb, win_pages)
    new_win = kv_win.reshape(nb, n_new, kvw)
    o = _decode("win", win_pt, qbd, slope_rows, scale_win, o,
                _pad_rows(new_win[:, :, :kw], PAGE), _pad_rows(new_win[:, :, kw:], PAGE), None,
                win_t, None, n_pp=win_pages, n_new=n_new, past=past, key_base=past - wb)
    shape5 = (nb, n_new, 2, N_KV_A, HEAD_DIM)
    win_all = jnp.concatenate([cache_win, kv_win.reshape(shape5)], axis=1)[:, n_new:]
    return (o.reshape(nb * n_new, d), kv_cmp.reshape(shape5), kv_sel.reshape(shape5), win_all)


def _fox_weights(b_w_in, b_q_norm, b_k_norm, b_f_bias):
    d = N_HEADS * HEAD_DIM
    w = b_w_in.astype(BF16)
    return dict(wq=w[:, :d], wk=w[:, d:2 * d], wv=w[:, 2 * d:3 * d], wf=_pad_cols(w[:, 3 * d:], LANES),
                qn=_tile_heads(b_q_norm, N_HEADS), kn=_tile_heads(b_k_norm, N_HEADS),
                fb=_pad_cols(b_f_bias[None, :].astype(F32), LANES))


def _fox_prompt(xp, g_mix, wts, tm, tq, tk):
    bd = _seg_matrix()
    q, kv, lf, kv_b = _proj_b(xp, g_mix, wts["wq"], wts["wk"], wts["wv"], wts["wf"], bd,
                              wts["qn"], wts["kn"], wts["fb"], tm)
    logf = lf[:, :N_HEADS]
    ct2 = _cumsum_t(logf.T, min(512, xp.shape[0]))
    o = _flash("fox", q, kv_b, (ct2,), tq, tk)
    return o, kv, logf


def _fox_sample(xs, g_mix, wts, cache_kv, cache_logf, page_table, tm):
    nb, n_new, d = xs.shape
    n_pages = page_table.shape[1]
    past = n_pages * PAGE
    bd = _seg_matrix()
    q, kv, lf, _ = _proj_b(xs.reshape(nb * n_new, d), g_mix, wts["wq"], wts["wk"], wts["wv"], wts["wf"], bd,
                           wts["qn"], wts["kn"], wts["fb"], tm)
    logf = lf[:, :N_HEADS]
    n_rows = n_new * N_HEADS
    qbd = _block_diag_q(q.reshape(nb, n_new, N_HEADS, HEAD_DIM), N_HEADS)
    lfn = jnp.transpose(logf.reshape(nb, n_new, N_HEADS), (0, 2, 1))
    lfn = jnp.tile(jnp.pad(lfn, ((0, 0), (0, 0), (0, PAGE - n_new))), (1, n_new, 1))
    ones = jnp.ones((nb, n_rows, 1), F32)
    zeros = jnp.zeros((nb, n_rows, HEAD_DIM), F32)
    new_kv = kv.reshape(nb, n_new, 2 * d)
    o = _decode("fox", page_table, qbd, jnp.zeros((n_rows, 1), F32), ones, zeros,
                _pad_rows(new_kv[:, :, :d], PAGE), _pad_rows(new_kv[:, :, d:], PAGE), lfn,
                _pages_t(cache_kv), jnp.transpose(cache_logf, (0, 2, 1)),
                n_pp=min(8, n_pages), n_new=n_new, past=past, key_base=0)
    return (o.reshape(nb * n_new, d), kv.reshape(nb, n_new, 2, N_HEADS, HEAD_DIM),
            logf.reshape(nb, n_new, N_HEADS))


def kernel(x_prompt, x_sample, cache_a_cmp_kv, cache_a_sel_kv, cache_a_win_kv, cache_b_kv, cache_b_logf,
           page_table, norm_mix, norm_ffn, a_w_in, a_q_norm, a_k_norm, a_gate_b, a_cmp_pos, a_cmp_w1,
           a_cmp_w2, a_w_out, b_w_in, b_q_norm, b_k_norm, b_f_bias, b_w_out, f_w_gate, f_w_up, f_w_down,
           m_router, m_w_gate, m_w_up, m_w_down):
    batch, seq, d = x_prompt.shape
    nb, n_new, _ = x_sample.shape
    assert batch == 1
    ts = nb * n_new
    tm_p = min(512, seq)
    tq_cmp = min(256, seq)
    tq = min(512, seq)
    tk = min(512, seq)
    tm_f = min(1024, seq)
    tf = 256
    xp = x_prompt.reshape(seq, d)
    xs = x_sample.reshape(ts, d)
    gm = norm_mix.astype(F32)[:, None, :]
    gf = norm_ffn.astype(F32)[:, None, :]

    wa = _nsa_weights(a_w_in[0], a_q_norm[0], a_k_norm[0], a_gate_b[0], a_cmp_pos[0], a_cmp_w1[0],
                      a_cmp_w2[0], d)
    op, cmp_p, sel_p, win_p = _nsa_prompt(xp, gm[0], wa, tm_p, tq_cmp, tq, tk)
    os_, cmp_s, sel_s, win_s = _nsa_sample(x_sample, gm[0], wa, cache_a_cmp_kv[0], cache_a_sel_kv[0],
                                           cache_a_win_kv[0], page_table, ts)
    w_out = a_w_out[0].astype(BF16)
    xp = _resid_proj(xp, op, w_out, tm_p)
    xs = _resid_proj(xs, os_, w_out, ts)
    fg, fu, fd = f_w_gate[0].astype(BF16), f_w_up[0].astype(BF16), f_w_down[0].astype(BF16)
    xp = _ffn(xp, gf[0], fg, fu, fd, tm_f, tf)
    xs = _ffn(xs, gf[0], fg, fu, fd, ts, tf)

    wb = _fox_weights(b_w_in[0], b_q_norm[0], b_k_norm[0], b_f_bias[0])
    op, kv_p, lf_p = _fox_prompt(xp, gm[1], wb, tm_p, tq, tk)
    os_, kv_s, lf_s = _fox_sample(xs.reshape(nb, n_new, d), gm[1], wb, cache_b_kv[0], cache_b_logf[0],
                                  page_table, ts)
    w_out = b_w_out[0].astype(BF16)
    xp = _resid_proj(xp, op, w_out, tm_p)
    xs = _resid_proj(xs, os_, w_out, ts)
    router = _pad_cols(m_router[0].astype(BF16), LANES)
    mg, mu, md = m_w_gate[0].astype(BF16), m_w_up[0].astype(BF16), m_w_down[0].astype(BF16)
    xp = _moe(xp, gf[1], router, mg, mu, md, tm_f, tf)
    xs = _moe(xs, gf[1], router, mg, mu, md, ts, tf)

    wlen = min(WINDOW, seq)
    kv5 = lambda a, rows: a.reshape(1, 1, rows, 2, N_KV_A, HEAD_DIM)
    return (xp.reshape(batch, seq, d), xs.reshape(nb, n_new, d),
            kv5(cmp_p, seq), cmp_s[None],
            kv5(sel_p, seq), sel_s[None],
            kv5(win_p[seq - wlen:], wlen), win_s[None],
            kv_p.reshape(1, 1, seq, 2, N_HEADS, HEAD_DIM), kv_s[None],
            lf_p.reshape(1, 1, seq, N_HEADS), lf_s[None])
```

```python
import functools

import numpy as np
import jax
import jax.numpy as jnp
from jax import lax
from jax.experimental import pallas as pl
from jax.experimental.pallas import tpu as pltpu

F32 = jnp.float32
BF16 = jnp.bfloat16

HEAD_DIM = 64
N_HEADS = 16
N_KV_A = 4
GROUP_A = N_HEADS // N_KV_A
CMP_BLOCK = 32
CMP_STRIDE = 16
CMP_HIDDEN = 256
SEL_BLOCK = 64
TOP_N = 16
WINDOW = 512
N_EXPERTS = 8
PAGE = 128
RMS_EPS = 1e-6
FORCE_SCORE = 1e4
ATTN_SCALE = HEAD_DIM ** -0.5
LOG2E = 1.4426950408889634
LANES = 128
SEG_CHUNK = 256
ROW_CHUNK = 64

NEG = -1e30
PICKED = -3e38
VMEM_LIMIT = 56 * 1024 * 1024


def _alibi_slopes():
    return (2.0 ** (-8.0 * np.arange(1, N_HEADS + 1) / N_HEADS)).astype(np.float32)


def _slopes_log2():
    return tuple(float(np.float32(s) * np.float32(LOG2E)) for s in _alibi_slopes())


def _dot(a, b):
    return jnp.dot(a, b, preferred_element_type=F32)


def _dot_nt(a, b):
    return lax.dot_general(a, b, (((1,), (1,)), ((), ())), preferred_element_type=F32)


def _dot3(x, m):
    x1 = x.astype(BF16)
    r1 = x - x1.astype(F32)
    x2 = r1.astype(BF16)
    x3 = (r1 - x2.astype(F32)).astype(BF16)
    return _dot(x1, m) + _dot(x2, m) + _dot(x3, m)


def _rms(x, g):
    y = x * lax.rsqrt(jnp.mean(x * x, axis=-1, keepdims=True) + RMS_EPS)
    return y * g


def _seg_norm(y, bd, g):
    outs = []
    for c in range(y.shape[1] // SEG_CHUNK):
        yc = y[:, c * SEG_CHUNK:(c + 1) * SEG_CHUNK]
        ms = _dot3(yc * yc, bd)
        outs.append(yc * lax.rsqrt(ms + RMS_EPS))
    out = outs[0] if len(outs) == 1 else jnp.concatenate(outs, axis=1)
    return out * g


def _silu(x):
    return x * jax.nn.sigmoid(x)


def _cparams(sem):
    return pltpu.CompilerParams(dimension_semantics=sem, vmem_limit_bytes=VMEM_LIMIT)


def _seg_matrix():
    i = np.arange(SEG_CHUNK)
    m = (i[:, None] // HEAD_DIM == i[None, :] // HEAD_DIM).astype(np.float32) / HEAD_DIM
    return jnp.asarray(m, BF16)


def _proj_a_kernel(x_ref, g_ref, wq_ref, wkv_ref, wg_ref, bd_ref, qn_ref, kn1_ref, kn2_ref, gb_ref,
                   q_out, cmp_out, sel_out, win_out, gate_out, selb_out, winb_out):
    h = _rms(x_ref[...], g_ref[...]).astype(BF16)
    bd = bd_ref[...]
    q_out[...] = _seg_norm(_dot(h, wq_ref[...]), bd, qn_ref[...])
    kv = _dot(h, wkv_ref[...])
    kvw = 2 * N_KV_A * HEAD_DIM
    kw = N_KV_A * HEAD_DIM
    cmp_out[...] = kv[:, :kvw]
    for base, kn_ref, out, outb in ((kvw, kn1_ref, sel_out, selb_out), (2 * kvw, kn2_ref, win_out, winb_out)):
        kn = _seg_norm(kv[:, base:base + kw], bd, kn_ref[...])
        v = kv[:, base + kw:base + kvw]
        out[:, :kw] = kn
        out[:, kw:] = v
        outb[:, :kw] = kn.astype(BF16)
        outb[:, kw:] = v.astype(BF16)
    gate_out[...] = jax.nn.sigmoid(_dot(h, wg_ref[...]) + gb_ref[...])


def _proj_a(x, g, wq, wkv, wg, bd, qn, kn1, kn2, gb, tm):
    t, d = x.shape
    kvw = 2 * N_KV_A * HEAD_DIM
    const = lambda shape: pl.BlockSpec(shape, lambda i: (0, 0))
    row = lambda w: pl.BlockSpec((tm, w), lambda i: (i, 0))
    return pl.pallas_call(
        _proj_a_kernel,
        grid=(t // tm,),
        in_specs=[row(d), const((1, d)), const(wq.shape), const(wkv.shape), const(wg.shape),
                  const(bd.shape), const(qn.shape), const(kn1.shape), const(kn2.shape), const(gb.shape)],
        out_specs=[row(d), row(kvw), row(kvw), row(kvw), row(LANES), row(kvw), row(kvw)],
        out_shape=[jax.ShapeDtypeStruct((t, d), F32)] + [jax.ShapeDtypeStruct((t, kvw), F32)] * 3
                  + [jax.ShapeDtypeStruct((t, LANES), F32)] + [jax.ShapeDtypeStruct((t, kvw), BF16)] * 2,
        compiler_params=_cparams(("parallel",)),
        name="proj_a",
    )(x, g, wq, wkv, wg, bd, qn, kn1, kn2, gb)


def _proj_b_kernel(x_ref, g_ref, wq_ref, wk_ref, wv_ref, wf_ref, bd_ref, qn_ref, kn_ref, fb_ref,
                   q_out, kv_out, lf_out, kvb_out):
    h = _rms(x_ref[...], g_ref[...]).astype(BF16)
    bd = bd_ref[...]
    d = q_out.shape[1]
    q_out[...] = _seg_norm(_dot(h, wq_ref[...]), bd, qn_ref[...])
    kn = _seg_norm(_dot(h, wk_ref[...]), bd, kn_ref[...])
    v = _dot(h, wv_ref[...])
    kv_out[:, :d] = kn
    kv_out[:, d:] = v
    kvb_out[:, :d] = kn.astype(BF16)
    kvb_out[:, d:] = v.astype(BF16)
    z = _dot(h, wf_ref[...]) + fb_ref[...]
    lf_out[...] = jnp.minimum(z, 0.0) - jnp.log1p(jnp.exp(-jnp.abs(z)))


def _proj_b(x, g, wq, wk, wv, wf, bd, qn, kn, fb, tm):
    t, d = x.shape
    const = lambda shape: pl.BlockSpec(shape, lambda i: (0, 0))
    row = lambda w: pl.BlockSpec((tm, w), lambda i: (i, 0))
    return pl.pallas_call(
        _proj_b_kernel,
        grid=(t // tm,),
        in_specs=[row(d), const((1, d)), const(wq.shape), const(wk.shape), const(wv.shape), const(wf.shape),
                  const(bd.shape), const(qn.shape), const(kn.shape), const(fb.shape)],
        out_specs=[row(d), row(2 * d), row(LANES), row(2 * d)],
        out_shape=[jax.ShapeDtypeStruct((t, d), F32), jax.ShapeDtypeStruct((t, 2 * d), F32),
                   jax.ShapeDtypeStruct((t, LANES), F32), jax.ShapeDtypeStruct((t, 2 * d), BF16)],
        compiler_params=_cparams(("parallel",)),
        name="proj_b",
    )(x, g, wq, wk, wv, wf, bd, qn, kn, fb)


def _ffn_kernel(x_ref, a_ref, wo_ref, g_ref, wg_ref, wu_ref, wd_ref, o_ref, xm_ref, xn_ref, acc_ref):
    j = pl.program_id(1)

    @pl.when(j == 0)
    def _():
        xm = x_ref[...] + _dot(a_ref[...].astype(BF16), wo_ref[...])
        xm_ref[...] = xm
        xn_ref[...] = _rms(xm, g_ref[...]).astype(BF16)
        acc_ref[...] = jnp.zeros_like(acc_ref)

    xn = xn_ref[...]
    hmid = _silu(_dot(xn, wg_ref[...])) * _dot(xn, wu_ref[...])
    acc_ref[...] += _dot(hmid.astype(BF16), wd_ref[...])

    @pl.when(j == pl.num_programs(1) - 1)
    def _():
        o_ref[...] = xm_ref[...] + acc_ref[...]


def _ffn(x, a, wo, g, wg, wu, wd, tm, tf):
    t, d = x.shape
    dff = wg.shape[1]
    row = pl.BlockSpec((tm, d), lambda i, j: (i, 0))
    return pl.pallas_call(
        _ffn_kernel,
        grid=(t // tm, dff // tf),
        in_specs=[row, row, pl.BlockSpec(wo.shape, lambda i, j: (0, 0)),
                  pl.BlockSpec((1, d), lambda i, j: (0, 0)),
                  pl.BlockSpec((d, tf), lambda i, j: (0, j)),
                  pl.BlockSpec((d, tf), lambda i, j: (0, j)),
                  pl.BlockSpec((tf, d), lambda i, j: (j, 0))],
        out_specs=row,
        out_shape=jax.ShapeDtypeStruct((t, d), F32),
        scratch_shapes=[pltpu.VMEM((tm, d), F32), pltpu.VMEM((tm, d), BF16), pltpu.VMEM((tm, d), F32)],
        compiler_params=_cparams(("parallel", "arbitrary")),
        name="ffn",
    )(x, a, wo, g, wg, wu, wd)


def _moe_kernel(x_ref, a_ref, wo_ref, g_ref, r_ref, wg_ref, wu_ref, wd_ref, o_ref,
                xm_ref, xn_ref, gate_ref, acc_ref):
    e = pl.program_id(1)
    j = pl.program_id(2)
    last_j = pl.num_programs(2) - 1

    @pl.when((e == 0) & (j == 0))
    def _():
        xm = x_ref[...] + _dot(a_ref[...].astype(BF16), wo_ref[...])
        xm_ref[...] = xm
        xn = _rms(xm, g_ref[...]).astype(BF16)
        xn_ref[...] = xn
        acc_ref[...] = jnp.zeros_like(acc_ref)
        logits = _dot(xn, r_ref[...])
        col = lax.broadcasted_iota(jnp.int32, logits.shape, 1).astype(F32)
        real = col < N_EXPERTS
        logits = jnp.where(real, logits, NEG)
        mx = jnp.max(logits, axis=-1, keepdims=True)
        pe = jnp.where(real, jnp.exp(logits - mx), 0.0)
        probs = pe / jnp.sum(pe, axis=-1, keepdims=True)
        work = jnp.where(real, probs, PICKED)
        picked = jnp.zeros(logits.shape, jnp.bool_)
        for _ in range(2):
            top = jnp.max(work, axis=-1, keepdims=True)
            idx = jnp.min(jnp.where(work == top, col, float(LANES)), axis=-1, keepdims=True)
            hit = col == idx
            picked = picked | hit
            work = jnp.where(hit, PICKED, work)
        sel = jnp.where(picked, probs, 0.0)
        gate_ref[...] = sel / jnp.sum(sel, axis=-1, keepdims=True)

    xn = xn_ref[...]
    hmid = _silu(_dot(xn, wg_ref[...])) * _dot(xn, wu_ref[...])
    gate = gate_ref[...]
    col = lax.broadcasted_iota(jnp.int32, gate.shape, 1)
    gcol = jnp.sum(jnp.where(col == e, gate, 0.0), axis=-1, keepdims=True)
    acc_ref[...] += gcol * _dot(hmid.astype(BF16), wd_ref[...])

    @pl.when((e == pl.num_programs(1) - 1) & (j == last_j))
    def _():
        o_ref[...] = xm_ref[...] + acc_ref[...]


def _moe(x, a, wo, g, router, wg, wu, wd, tm, tf):
    t, d = x.shape
    ne, _, dff = wg.shape
    row = pl.BlockSpec((tm, d), lambda i, e, j: (i, 0))
    return pl.pallas_call(
        _moe_kernel,
        grid=(t // tm, ne, dff // tf),
        in_specs=[row, row, pl.BlockSpec(wo.shape, lambda i, e, j: (0, 0)),
                  pl.BlockSpec((1, d), lambda i, e, j: (0, 0)),
                  pl.BlockSpec(router.shape, lambda i, e, j: (0, 0)),
                  pl.BlockSpec((None, d, tf), lambda i, e, j: (e, 0, j)),
                  pl.BlockSpec((None, d, tf), lambda i, e, j: (e, 0, j)),
                  pl.BlockSpec((None, tf, d), lambda i, e, j: (e, j, 0))],
        out_specs=row,
        out_shape=jax.ShapeDtypeStruct((t, d), F32),
        scratch_shapes=[pltpu.VMEM((tm, d), F32), pltpu.VMEM((tm, d), BF16), pltpu.VMEM((tm, LANES), F32),
                        pltpu.VMEM((tm, d), F32)],
        compiler_params=_cparams(("parallel", "arbitrary", "arbitrary")),
        name="moe",
    )(x, a, wo, g, router, wg, wu, wd)


ROWS_PER_PAGE = PAGE // CMP_STRIDE


def _compress_kernel(pt_ref, x_hbm, pos_ref, w1_ref, w2_ref, kn_ref, o_ref, xbuf, sem, *, n_pages, n_kvg):
    b = pl.program_id(0)
    c = pl.program_id(1)
    step = b * n_kvg + c
    n_steps = pl.num_programs(0) * n_kvg
    slot = step % 2

    def page_copy(bb, cc, sl, p):
        return pltpu.make_async_copy(
            x_hbm.at[pt_ref[bb, p], cc],
            xbuf.at[sl, pl.ds(p * ROWS_PER_PAGE, ROWS_PER_PAGE), :],
            sem.at[sl])

    def fetch(bb, cc, sl):
        def body(p, carry):
            page_copy(bb, cc, sl, p).start()
            return carry
        lax.fori_loop(0, n_pages, body, 0)

    @pl.when(step == 0)
    def _():
        fetch(b, c, slot)

    @pl.when(step + 1 < n_steps)
    def _():
        nxt = step + 1
        fetch(nxt // n_kvg, nxt % n_kvg, 1 - slot)

    def wait_body(p, carry):
        page_copy(b, c, slot, p).wait()
        return carry
    lax.fori_loop(0, n_pages, wait_body, 0)

    x = xbuf[slot]
    n_rows = x.shape[0]
    z0 = _dot((x + pos_ref[0]).astype(BF16), w1_ref[0])
    z1 = _dot((x + pos_ref[1]).astype(BF16), w1_ref[1])
    hid = z0 + pltpu.roll(z1, n_rows - 1, 0)
    y = _dot(_silu(hid).astype(BF16), w2_ref[...])
    yn = _rms(y, kn_ref[...])
    is_key = c < (n_kvg // 2)
    o_ref[...] = jnp.where(is_key, yn, y)


def _compress(pt, xr, posr, w1r, w2, kn):
    nb, n_pages = pt.shape
    n_kvg = xr.shape[1]
    half = n_kvg // 2
    n_rows = n_pages * ROWS_PER_PAGE
    width = xr.shape[3]
    grid_spec = pltpu.PrefetchScalarGridSpec(
        num_scalar_prefetch=1,
        grid=(nb, n_kvg),
        in_specs=[pl.BlockSpec(memory_space=pl.ANY),
                  pl.BlockSpec((None, 2, 1, width), lambda b, c, pt: (c // half, 0, 0, 0)),
                  pl.BlockSpec((None, 2, width, CMP_HIDDEN), lambda b, c, pt: (c // half, 0, 0, 0)),
                  pl.BlockSpec((None, CMP_HIDDEN, HEAD_DIM), lambda b, c, pt: (c // half, 0, 0)),
                  pl.BlockSpec((1, HEAD_DIM), lambda b, c, pt: (0, 0))],
        out_specs=pl.BlockSpec((None, None, n_rows, HEAD_DIM), lambda b, c, pt: (b, c, 0, 0)),
        scratch_shapes=[pltpu.VMEM((2, n_rows, width), F32), pltpu.SemaphoreType.DMA((2,))],
    )
    return pl.pallas_call(
        functools.partial(_compress_kernel, n_pages=n_pages, n_kvg=n_kvg),
        grid_spec=grid_spec,
        out_shape=jax.ShapeDtypeStruct((nb, n_kvg, n_rows, HEAD_DIM), F32),
        compiler_params=_cparams(("arbitrary", "arbitrary")),
        name="compress",
    )(pt, xr, posr, w1r, w2, kn)


def _cmp_topk_kernel(q_ref, pos_ref, kc_ref, vc_ref, ov_ref, gate_ref, o_ref, sb_ref, *,
                     tq, n_cmp, n_sel, n_top, slopes2):
    nc = kc_ref.shape[0]
    ns = ov_ref.shape[1]
    q = q_ref[...]
    pos = pos_ref[...]
    gates = gate_ref[...]
    n_idx = lax.broadcasted_iota(jnp.int32, (1, nc), 1)
    dcmp = pos - (n_idx * CMP_STRIDE + (CMP_BLOCK - 1))
    mask = (dcmp >= 0) & (n_idx < n_cmp)
    rel = (n_idx * CMP_STRIDE + (CMP_BLOCK - 1) - pos[0:1, :]).astype(F32)
    blk = lax.broadcasted_iota(jnp.int32, (1, ns), 1)
    blkf = blk.astype(F32)
    cur = pos // SEL_BLOCK
    forced = (blk == 0) | (blk == cur) | (blk == cur - 1)
    valid = (blk * SEL_BLOCK <= pos) & (blk < n_sel)
    ov = ov_ref[...]
    for g in range(N_KV_A):
        heads = range(g * GROUP_A, (g + 1) * GROUP_A)
        qg = (jnp.concatenate([q[:, h * HEAD_DIM:(h + 1) * HEAD_DIM] for h in heads], axis=0)
              * (ATTN_SCALE * LOG2E)).astype(BF16)
        kg = kc_ref[:, g * HEAD_DIM:(g + 1) * HEAD_DIM].astype(BF16)
        vg = vc_ref[:, g * HEAD_DIM:(g + 1) * HEAD_DIM].astype(BF16)
        s = _dot_nt(qg, kg)
        psum = jnp.zeros((tq, nc), F32)
        for r, h in enumerate(heads):
            sr = jnp.where(mask, s[r * tq:(r + 1) * tq] + slopes2[h] * rel, NEG)
            mx = jnp.max(sr, axis=-1, keepdims=True)
            p = jnp.where(mask, jnp.exp2(sr - mx), 0.0)
            p = p * (1.0 / jnp.maximum(jnp.sum(p, axis=-1, keepdims=True), 1e-30))
            o = _dot(p.astype(BF16), vg)
            o_ref[:, h * HEAD_DIM:(h + 1) * HEAD_DIM] = gates[:, h:h + 1] * o
            psum = psum + p
        imp = _dot(psum.astype(BF16), ov)
        imp = jnp.where(forced, FORCE_SCORE, imp)
        work = jnp.where(valid, imp, NEG)
        work = jnp.where(blk < n_sel, work, PICKED)
        picked = jnp.zeros((tq, ns), jnp.bool_)
        for _ in range(n_top):
            top = jnp.max(work, axis=-1, keepdims=True)
            idx = jnp.min(jnp.where(work == top, blkf, float(ns)), axis=-1, keepdims=True)
            hit = blkf == idx
            picked = picked | hit
            work = jnp.where(hit, PICKED, work)
        sb_ref[:, g * ns:(g + 1) * ns] = jnp.where(picked, 0.0, NEG).astype(BF16)


def _cmp_topk(q, pos, kc, vc, ov, gates, tq, tiles_per_seq, n_cmp, n_sel):
    t, d = q.shape
    nc = kc.shape[1]
    ns = ov.shape[1]
    kvw = N_KV_A * HEAD_DIM
    row = lambda w: pl.BlockSpec((tq, w), lambda i: (i, 0))
    seq = pl.BlockSpec((None, nc, kvw), lambda i: (i // tiles_per_seq, 0, 0))
    return pl.pallas_call(
        functools.partial(_cmp_topk_kernel, tq=tq, n_cmp=n_cmp, n_sel=n_sel,
                          n_top=min(TOP_N, n_sel), slopes2=_slopes_log2()),
        grid=(t // tq,),
        in_specs=[row(d), row(1), seq, seq, pl.BlockSpec(ov.shape, lambda i: (0, 0)), row(LANES)],
        out_specs=[row(d), row(N_KV_A * ns)],
        out_shape=[jax.ShapeDtypeStruct((t, d), F32), jax.ShapeDtypeStruct((t, N_KV_A * ns), BF16)],
        compiler_params=_cparams(("parallel",)),
        name="cmp_topk",
    )(q, pos, kc, vc, ov, gates)


def _flash_kernel(qi_ref, ki_ref, first_ref, last_ref, *refs, kind, kv_heads, tq, tk, slopes2, gate_off):
    qs_ref, m_ref, l_ref, acc_ref, p_ref = refs[-5:]
    if kind == "fox":
        q_ref, kv_ref, ct_ref, o_ref = refs[:-5]
    elif kind == "sel":
        q_ref, kv_ref, sb_ref, prev_ref, gate_ref, o_ref = refs[:-5]
    else:
        q_ref, kv_ref, prev_ref, gate_ref, o_ref = refs[:-5]
    rep = N_HEADS // kv_heads
    v_off = kv_heads * HEAD_DIM
    s_id = pl.program_id(0)
    kt = ki_ref[s_id]
    q0 = qi_ref[s_id] * tq
    low = lax.broadcasted_iota(jnp.int32, (1, LANES), 1) < HEAD_DIM

    def kv_half(h):
        return (h // rep) % 2

    @pl.when(first_ref[s_id] == 1)
    def _():
        m_ref[...] = jnp.full(m_ref.shape, NEG, F32)
        l_ref[...] = jnp.zeros_like(l_ref)
        acc_ref[...] = jnp.zeros_like(acc_ref)
        for j in range(N_HEADS // 2):
            q2 = q_ref[:, j * LANES:(j + 1) * LANES] * (ATTN_SCALE * LOG2E)
            for e in range(2):
                h = 2 * j + e
                mine = jnp.where(low if e == 0 else jnp.logical_not(low), q2, 0.0)
                if kv_half(h) != e:
                    mine = pltpu.roll(mine, HEAD_DIM, 1)
                qs_ref[h * tq:(h + 1) * tq, :] = mine.astype(BF16)

    def step(masked):
        kpos = kt * tk + lax.broadcasted_iota(jnp.int32, (1, tk), 1)
        if kind != "fox":
            rel = (kpos - q0).astype(F32)
        if kind == "sel":
            sbt = sb_ref[...]
            srow = lax.broadcasted_iota(jnp.int32, (sbt.shape[1], 1), 0)
            blk_in_tile = lax.broadcasted_iota(jnp.int32, (1, tk), 1) // SEL_BLOCK
            per_tile = tk // SEL_BLOCK
        for g in range(kv_heads):
            pair = g // 2
            k2 = kv_ref[:, pair * LANES:(pair + 1) * LANES]
            v2 = kv_ref[:, v_off + pair * LANES:v_off + (pair + 1) * LANES]
            if kind == "fox":
                brow = -ct_ref[g:g + 1, :]
            if kind == "sel":
                selb = _dot(sbt, (srow == g * per_tile + blk_in_tile).astype(BF16))
            for r in range(rep):
                h = g * rep + r
                hr = slice(h * tq, (h + 1) * tq)
                s = _dot_nt(qs_ref[hr, :], k2)
                if kind != "fox":
                    brow = slopes2[h] * rel
                for c in range(tq // ROW_CHUNK):
                    cr = slice(c * ROW_CHUNK, (c + 1) * ROW_CHUNK)
                    hc = slice(h * tq + c * ROW_CHUNK, h * tq + (c + 1) * ROW_CHUNK)
                    sc = s[cr] + brow
                    if kind == "sel":
                        sc = sc + selb[cr]
                    if masked:
                        tpos = q0 + c * ROW_CHUNK + lax.broadcasted_iota(jnp.int32, (ROW_CHUNK, 1), 0)
                        dist = tpos - kpos
                        keep = (dist >= 0) & (dist < WINDOW) if kind == "win" else dist >= 0
                        sc = jnp.where(keep, sc, NEG)
                    m_prev = m_ref[hc, :]
                    m_new = jnp.maximum(m_prev, jnp.max(sc, axis=-1, keepdims=True))
                    alpha = jnp.exp2(m_prev - m_new)
                    pc = jnp.exp2(sc - jnp.tile(m_new, (1, tk // LANES)))
                    l_ref[hc, :] = alpha * l_ref[hc, :] + jnp.sum(pc, axis=-1, keepdims=True)
                    acc_ref[hc, :] = alpha * acc_ref[hc, :]
                    m_ref[hc, :] = m_new
                    p_ref[h % 2, cr, :] = pc.astype(BF16)
                acc_ref[hr, :] += _dot(p_ref[h % 2], v2)

    if kind == "win":
        step(True)
    else:
        crosses = (kt + 1) * tk - 1 > q0
        pl.when(crosses)(lambda: step(True))
        pl.when(jnp.logical_not(crosses))(lambda: step(False))

    @pl.when(last_ref[s_id] == 1)
    def _():
        for j in range(N_HEADS // 2):
            halves = []
            for e in range(2):
                h = 2 * j + e
                hr = slice(h * tq, (h + 1) * tq)
                o = acc_ref[hr, :] / l_ref[hr, :]
                if kv_half(h) != e:
                    o = pltpu.roll(o, HEAD_DIM, 1)
                halves.append(o)
            o2 = jnp.where(low, halves[0], halves[1])
            cols = slice(j * LANES, (j + 1) * LANES)
            if kind == "fox":
                o_ref[:, cols] = o2
            else:
                c0 = gate_off + 2 * j
                g2 = jnp.where(low, gate_ref[:, c0:c0 + 1], gate_ref[:, c0 + 1:c0 + 2])
                o_ref[:, cols] = prev_ref[:, cols] + g2 * o2


def _flash(kind, q, kv, extras, tq, tk):
    t, d = q.shape
    kv_heads = kv.shape[1] // (2 * HEAD_DIM)
    nq = t // tq
    qi, ki, first, last = [], [], [], []
    for i in range(nq):
        hi = (i * tq + tq - 1) // tk
        lo = max(i * tq - (WINDOW - 1), 0) // tk if kind == "win" else 0
        for kt in range(lo, hi + 1):
            qi.append(i)
            ki.append(kt)
            first.append(int(kt == lo))
            last.append(int(kt == hi))
    tables = [jnp.asarray(np.asarray(a, np.int32)) for a in (qi, ki, first, last)]

    qrow = lambda w: pl.BlockSpec((tq, w), lambda s, qi, ki, fi, la: (qi[s], 0))
    in_specs = [qrow(d), pl.BlockSpec((tk, kv.shape[1]), lambda s, qi, ki, fi, la: (ki[s], 0))]
    if kind == "fox":
        (ct2,) = extras
        in_specs += [pl.BlockSpec((ct2.shape[0], tk), lambda s, qi, ki, fi, la: (0, ki[s]))]
        gate_off = 0
    elif kind == "sel":
        sbt, prev, gates = extras
        in_specs += [pl.BlockSpec((None, tq, sbt.shape[2]), lambda s, qi, ki, fi, la: (ki[s], qi[s], 0)),
                     qrow(d), qrow(LANES)]
        gate_off = N_HEADS
    else:
        prev, gates = extras
        in_specs += [qrow(d), qrow(LANES)]
        gate_off = 2 * N_HEADS
    grid_spec = pltpu.PrefetchScalarGridSpec(
        num_scalar_prefetch=4, grid=(len(qi),), in_specs=in_specs, out_specs=qrow(d),
        scratch_shapes=[pltpu.VMEM((N_HEADS * tq, LANES), BF16), pltpu.VMEM((N_HEADS * tq, LANES), F32),
                        pltpu.VMEM((N_HEADS * tq, LANES), F32), pltpu.VMEM((N_HEADS * tq, LANES), F32),
                        pltpu.VMEM((2, tq, tk), BF16)])
    return pl.pallas_call(
        functools.partial(_flash_kernel, kind=kind, kv_heads=kv_heads, tq=tq, tk=tk,
                          slopes2=_slopes_log2(), gate_off=gate_off),
        grid_spec=grid_spec,
        out_shape=jax.ShapeDtypeStruct((t, d), F32),
        compiler_params=_cparams(("arbitrary",)),
        name="flash_" + kind,
    )(*tables, q, kv, *extras)


def _cumsum_kernel(x_ref, o_ref, carry_ref):
    @pl.when(pl.program_id(0) == 0)
    def _():
        carry_ref[...] = jnp.zeros_like(carry_ref)

    x = x_ref[...]
    tm = x.shape[1]
    upper = (lax.broadcasted_iota(jnp.int32, (tm, tm), 0)
             <= lax.broadcasted_iota(jnp.int32, (tm, tm), 1)).astype(BF16)
    c = _dot3(x, upper) + carry_ref[:, 0:1]
    o_ref[...] = c * LOG2E
    carry_ref[...] = jnp.broadcast_to(c[:, tm - 1:tm], carry_ref.shape)


def _cumsum_t(xt, tm):
    nh, t = xt.shape
    blk = pl.BlockSpec((nh, tm), lambda i: (0, i))
    return pl.pallas_call(
        _cumsum_kernel,
        grid=(t // tm,),
        in_specs=[blk],
        out_specs=blk,
        out_shape=jax.ShapeDtypeStruct((nh, t), F32),
        scratch_shapes=[pltpu.VMEM((nh, LANES), F32)],
        compiler_params=_cparams(("arbitrary",)),
        name="cumsum",
    )(xt)


def _decode_kernel(*refs, kind, n_pp, n_pages, n_new, past, key_base):
    pt_ref = refs[0]
    qbd_ref, slope_ref, scale_ref, prev_ref, knew_ref, vnew_ref = refs[1:7]
    k = 7
    if kind == "fox":
        lfn_ref = refs[k]
        k += 1
    elif kind == "sel":
        sb_ref = refs[k]
        k += 1
    page_refs = refs[k:k + n_pp]
    k += n_pp
    if kind == "fox":
        lf_refs = refs[k:k + n_pp]
        k += n_pp
    o_ref = refs[k]
    m_ref, l_ref, acc_ref = refs[k + 1:k + 4]
    if kind == "fox":
        carry_ref = refs[k + 4]
    del pt_ref

    s_id = pl.program_id(1)
    n_rows, c_dim = qbd_ref.shape
    kv_heads = c_dim // HEAD_DIM
    row = lax.broadcasted_iota(jnp.int32, (n_rows, 1), 0)
    t_row = row // N_HEADS
    h_row = row % N_HEADS
    lane = lax.broadcasted_iota(jnp.int32, (1, PAGE), 1)
    qbd = qbd_ref[...]

    @pl.when(s_id == 0)
    def _():
        m_ref[...] = jnp.full(m_ref.shape, NEG, F32)
        l_ref[...] = jnp.zeros_like(l_ref)
        acc_ref[...] = jnp.zeros_like(acc_ref)
        if kind == "fox":
            carry_ref[...] = jnp.zeros_like(carry_ref)

    if kind == "fox":
        incl = (lax.broadcasted_iota(jnp.int32, (PAGE, PAGE), 0)
                <= lax.broadcasted_iota(jnp.int32, (PAGE, PAGE), 1)).astype(BF16)
        cnu = _dot3(lfn_ref[...], incl)
        cn = jnp.sum(jnp.where(lane == t_row, cnu, 0.0), axis=-1, keepdims=True)
        after = (lax.broadcasted_iota(jnp.int32, (PAGE, PAGE), 0)
                 > lax.broadcasted_iota(jnp.int32, (PAGE, PAGE), 1)).astype(BF16)
    else:
        slope = slope_ref[...]
        qpos = past + t_row

    def update(s, v_parts):
        m_prev = m_ref[...]
        m_new = jnp.maximum(m_prev, jnp.max(s, axis=-1, keepdims=True))
        alpha = jnp.exp(m_prev - m_new)
        p = jnp.exp(s - m_new)
        l_ref[...] = alpha * l_ref[...] + jnp.sum(p, axis=-1, keepdims=True)
        acc = alpha * acc_ref[...]
        for j, (v, transposed) in enumerate(v_parts):
            pj = p[:, j * PAGE:(j + 1) * PAGE].astype(BF16)
            acc = acc + (_dot_nt(pj, v) if transposed else _dot(pj, v))
        acc_ref[...] = acc
        m_ref[...] = m_new

    s_parts, b_parts, v_parts = [], [], []
    for j in range(n_pp):
        page_idx = n_pages - 1 - (s_id * n_pp + j)
        kt = page_refs[j][0].astype(BF16)
        s_parts.append(_dot(qbd, kt))
        v_parts.append((page_refs[j][1].astype(BF16), True))
        if kind == "fox":
            lf = lf_refs[j][...]
            carry = carry_ref[:, 0:1]
            suf = _dot3(lf, after) + carry
            carry_ref[...] = jnp.broadcast_to(carry + jnp.sum(lf, axis=-1, keepdims=True), carry_ref.shape)
            b_parts.append(jnp.concatenate([suf] * (n_rows // N_HEADS), axis=0) + cn)
        else:
            kbuf = page_idx * PAGE + lane
            bias = -slope * (qpos - (key_base + kbuf)).astype(F32)
            if kind == "win":
                bias = jnp.where(kbuf > t_row, bias, NEG)
            b_parts.append(bias)
    s = jnp.concatenate(s_parts, axis=1) + jnp.concatenate(b_parts, axis=1)
    if kind == "sel":
        sb = sb_ref[...]
        ns = sb.shape[1]
        kpos = jnp.concatenate(
            [(n_pages - 1 - (s_id * n_pp + j)) * PAGE + lane for j in range(n_pp)], axis=1)
        expand = (lax.broadcasted_iota(jnp.int32, (ns, 1), 0) == kpos // SEL_BLOCK).astype(BF16)
        s = s + _dot(sb, expand)
    update(s, v_parts)

    @pl.when(s_id == pl.num_programs(1) - 1)
    def _():
        s_new = _dot_nt(qbd, knew_ref[...].astype(BF16))
        ok = (lane <= t_row) & (lane < n_new)
        if kind == "fox":
            bias = cn - cnu
        else:
            bias = -slope * (t_row - lane).astype(F32)
            if kind == "sel":
                ns = sb_ref.shape[1]
                expand_new = (lax.broadcasted_iota(jnp.int32, (ns, 1), 0)
                              == (past + lane) // SEL_BLOCK).astype(BF16)
                bias = bias + _dot(sb_ref[...], expand_new)
        update(jnp.where(ok, s_new + bias, NEG), [(vnew_ref[...].astype(BF16), False)])
        o_full = acc_ref[...] / l_ref[...]
        col_head = lax.broadcasted_iota(jnp.int32, (1, c_dim), 1) // HEAD_DIM
        own = col_head == h_row // (N_HEADS // kv_heads)
        o_full = jnp.where(own, o_full, 0.0)
        o = o_full[:, 0:HEAD_DIM]
        for c in range(1, kv_heads):
            o = o + o_full[:, c * HEAD_DIM:(c + 1) * HEAD_DIM]
        o_ref[...] = prev_ref[...] + scale_ref[...] * o


def _decode(kind, pt, qbd, slope_rows, scale_rows, prev, knew, vnew, extra, pages, lf_pages,
            n_pp, n_new, past, key_base):
    nb, n_pages = pt.shape
    n_rows, c_dim = qbd.shape[1:]
    n_steps = n_pages // n_pp
    seq = lambda shape: pl.BlockSpec((None,) + shape, lambda b, s, pt: (b,) + (0,) * len(shape))

    def page_map(j):
        return lambda b, s, pt: (pt[b, n_pages - 1 - (s * n_pp + j)], 0, 0, 0)

    def lf_map(j):
        return lambda b, s, pt: (pt[b, n_pages - 1 - (s * n_pp + j)], 0, 0)

    in_specs = [seq((n_rows, c_dim)), pl.BlockSpec((n_rows, 1), lambda b, s, pt: (0, 0)),
                seq((n_rows, 1)), seq((n_rows, HEAD_DIM)), seq((PAGE, c_dim)), seq((PAGE, c_dim))]
    args = [qbd, slope_rows, scale_rows, prev, knew, vnew]
    if extra is not None:
        in_specs.append(seq(extra.shape[1:]))
        args.append(extra)
    in_specs += [pl.BlockSpec((None, 2, c_dim, PAGE), page_map(j)) for j in range(n_pp)]
    args += [pages] * n_pp
    scratch = [pltpu.VMEM((n_rows, 1), F32), pltpu.VMEM((n_rows, 1), F32), pltpu.VMEM((n_rows, c_dim), F32)]
    if kind == "fox":
        in_specs += [pl.BlockSpec((None, N_HEADS, PAGE), lf_map(j)) for j in range(n_pp)]
        args += [lf_pages] * n_pp
        scratch.append(pltpu.VMEM((N_HEADS, LANES), F32))
    grid_spec = pltpu.PrefetchScalarGridSpec(
        num_scalar_prefetch=1, grid=(nb, n_steps), in_specs=in_specs,
        out_specs=seq((n_rows, HEAD_DIM)), scratch_shapes=scratch)
    return pl.pallas_call(
        functools.partial(_decode_kernel, kind=kind, n_pp=n_pp, n_pages=n_pages, n_new=n_new,
                          past=past, key_base=key_base),
        grid_spec=grid_spec,
        out_shape=jax.ShapeDtypeStruct((nb, n_rows, HEAD_DIM), F32),
        compiler_params=_cparams(("parallel", "arbitrary")),
        name="decode_" + kind,
    )(pt, *args)


def _overlap_matrix(n_cmp_pad, n_cmp, n_sel_pad, n_sel):
    cs = np.arange(n_cmp_pad)[:, None] * CMP_STRIDE
    ss = np.arange(n_sel_pad)[None, :] * SEL_BLOCK
    ov = np.minimum(cs + CMP_BLOCK, ss + SEL_BLOCK) - np.maximum(cs, ss)
    ov = np.clip(ov, 0, None) / CMP_BLOCK
    ov = ov * (np.arange(n_cmp_pad)[:, None] < n_cmp) * (np.arange(n_sel_pad)[None, :] < n_sel)
    return jnp.asarray(ov.astype(np.float32), BF16)


def _tile_heads(v, n):
    return jnp.tile(v.astype(F32), n)[None, :]


def _pad_cols(w, width):
    return jnp.pad(w, ((0, 0), (0, width - w.shape[1])))


def _block_diag_q(q, kv_heads):
    nb, t = q.shape[:2]
    own = (np.arange(N_HEADS)[:, None] // (N_HEADS // kv_heads) == np.arange(kv_heads)[None, :])
    qb = q[:, :, :, None, :] * jnp.asarray(own, F32)[None, None, :, :, None]
    return (qb * ATTN_SCALE).reshape(nb, t * N_HEADS, kv_heads * HEAD_DIM).astype(BF16)


def _pages_t(cache):
    pool, rows, two, heads, hd = cache.shape
    return jnp.transpose(cache, (0, 2, 3, 4, 1)).reshape(pool, two, heads * hd, rows)


def _pad_rows(x, rows):
    return jnp.pad(x, ((0, 0), (0, rows - x.shape[1]), (0, 0)))


def _nsa_weights(a_w_in, a_q_norm, a_k_norm, a_gate_b, a_cmp_pos, a_cmp_w1, a_cmp_w2, d):
    qc = N_HEADS * HEAD_DIM
    kvw = 2 * N_KV_A * HEAD_DIM
    w = a_w_in.astype(BF16)
    half = CMP_STRIDE * HEAD_DIM
    return dict(
        wq=w[:, :qc], wkv=w[:, qc:qc + 3 * kvw], wg=_pad_cols(w[:, qc + 3 * kvw:], LANES),
        qn=_tile_heads(a_q_norm, N_HEADS), kn0=a_k_norm[0][None, :].astype(F32),
        kn1=_tile_heads(a_k_norm[1], N_KV_A), kn2=_tile_heads(a_k_norm[2], N_KV_A),
        gb=_pad_cols(a_gate_b[None, :].astype(F32), LANES),
        posr=a_cmp_pos.reshape(2, 2, 1, half).astype(F32),
        w1r=a_cmp_w1.reshape(2, 2, half, CMP_HIDDEN).astype(BF16),
        w2=a_cmp_w2.astype(BF16))


def _row_groups(x, n_kvg):
    rows = x.shape[0]
    r = rows // CMP_STRIDE
    xr = x.reshape(r // ROWS_PER_PAGE, ROWS_PER_PAGE, CMP_STRIDE, n_kvg, HEAD_DIM)
    return jnp.transpose(xr, (0, 3, 1, 2, 4)).reshape(r // ROWS_PER_PAGE, n_kvg, ROWS_PER_PAGE,
                                                      CMP_STRIDE * HEAD_DIM)


def _split_kc_vc(cmp_out):
    nb, _, rows, _ = cmp_out.shape
    both = jnp.transpose(cmp_out.reshape(nb, 2, N_KV_A, rows, HEAD_DIM), (0, 1, 3, 2, 4))
    both = both.reshape(nb, 2, rows, N_KV_A * HEAD_DIM)
    return both[:, 0], both[:, 1]


def _nsa_prompt(xp, g_mix, wts, tm, tq_cmp, tq, tk):
    t, d = xp.shape
    bd = _seg_matrix()
    q, kv_cmp, kv_sel, kv_win, gates, sel_b, win_b = _proj_a(
        xp, g_mix, wts["wq"], wts["wkv"], wts["wg"], bd, wts["qn"], wts["kn1"], wts["kn2"], wts["gb"], tm)
    n_kvg = 2 * N_KV_A
    xr = _row_groups(kv_cmp, n_kvg)
    pt = jnp.arange(xr.shape[0], dtype=jnp.int32)[None, :]
    kc, vc = _split_kc_vc(_compress(pt, xr, wts["posr"], wts["w1r"], wts["w2"], wts["kn0"]))
    n_cmp = (t - CMP_BLOCK) // CMP_STRIDE + 1
    n_sel = -(-t // SEL_BLOCK)
    ns_pad = -(-n_sel // LANES) * LANES
    ov = _overlap_matrix(kc.shape[1], n_cmp, ns_pad, n_sel)
    pos = jnp.arange(t, dtype=jnp.int32)[:, None]
    o_cmp, sb = _cmp_topk(q, pos, kc, vc, ov, gates, tq_cmp, t // tq_cmp, n_cmp, n_sel)
    per_tile = tk // SEL_BLOCK
    sbt = sb.reshape(t, N_KV_A, ns_pad)[:, :, :(t // tk) * per_tile].reshape(t, N_KV_A, t // tk, per_tile)
    sbt = jnp.transpose(sbt, (2, 0, 1, 3)).reshape(t // tk, t, N_KV_A * per_tile)
    o = _flash("sel", q, sel_b, (sbt, o_cmp, gates), tq, tk)
    o = _flash("win", q, win_b, (o, gates), tq, tk)
    return o, kv_cmp, kv_sel, kv_win


def _nsa_sample(xs, g_mix, wts, cache_cmp, cache_sel, cache_win, page_table, tm):
    nb, n_new, d = xs.shape
    n_pages = page_table.shape[1]
    past = n_pages * PAGE
    bd = _seg_matrix()
    q, kv_cmp, kv_sel, kv_win, gates, _, _ = _proj_a(xs.reshape(nb * n_new, d), g_mix, wts["wq"], wts["wkv"],
                                                     wts["wg"], bd, wts["qn"], wts["kn1"], wts["kn2"],
                                                     wts["gb"], tm)
    n_kvg = 2 * N_KV_A
    kvw = n_kvg * HEAD_DIM
    kw = N_KV_A * HEAD_DIM
    pool = cache_cmp.shape[0]
    xr = _row_groups(cache_cmp.reshape(pool * PAGE, kvw), n_kvg)
    kc, vc = _split_kc_vc(_compress(page_table, xr, wts["posr"], wts["w1r"], wts["w2"], wts["kn0"]))
    total = past + n_new
    n_cmp = (total - CMP_BLOCK) // CMP_STRIDE + 1
    assert n_cmp <= kc.shape[1] and (n_cmp - 1) * CMP_STRIDE + CMP_BLOCK <= past
    n_sel = -(-total // SEL_BLOCK)
    ns_pad = -(-n_sel // LANES) * LANES
    ov = _overlap_matrix(kc.shape[1], n_cmp, ns_pad, n_sel)
    rows_pad = 8
    qp = _pad_rows(q.reshape(nb, n_new, d), rows_pad).reshape(nb * rows_pad, d)
    gp = _pad_rows(gates.reshape(nb, n_new, LANES), rows_pad).reshape(nb * rows_pad, LANES)
    pos = jnp.tile(past + jnp.minimum(jnp.arange(rows_pad, dtype=jnp.int32), n_new - 1), nb)[:, None]
    o_cmp, sb = _cmp_topk(qp, pos, kc, vc, ov, gp, rows_pad, 1, n_cmp, n_sel)
    n_rows = n_new * N_HEADS
    o_cmp = o_cmp.reshape(nb, rows_pad, d)[:, :n_new].reshape(nb, n_rows, HEAD_DIM)
    sb = sb.reshape(nb, rows_pad, N_KV_A, ns_pad)[:, :n_new]
    sb = jnp.repeat(sb, GROUP_A, axis=2).reshape(nb, n_rows, ns_pad)
    qbd = _block_diag_q(q.reshape(nb, n_new, N_HEADS, HEAD_DIM), N_KV_A)
    slope_rows = jnp.asarray(np.tile(_alibi_slopes(), n_new)[:, None])
    g3 = gates.reshape(nb, n_new, LANES)
    scale_sel = g3[:, :, N_HEADS:2 * N_HEADS].reshape(nb, n_rows, 1)
    scale_win = g3[:, :, 2 * N_HEADS:3 * N_HEADS].reshape(nb, n_rows, 1)
    new_sel = kv_sel.reshape(nb, n_new, kvw)
    o = _decode("sel", page_table, qbd, slope_rows, scale_sel, o_cmp,
                _pad_rows(new_sel[:, :, :kw], PAGE), _pad_rows(new_sel[:, :, kw:], PAGE), sb,
                _pages_t(cache_sel), None, n_pp=min(16, n_pages), n_new=n_new, past=past, key_base=0)
    wb = cache_win.shape[1]
    assert wb == WINDOW
    win_pages = wb // PAGE
    win_t = _pages_t(cache_win.reshape(nb * win_pages, PAGE, 2, N_KV_A, HEAD_DIM))
    win_pt = jnp.arange(nb * win_pages, dtype=jnp.int32).reshape(nb, win_pages)
    new_win = kv_win.reshape(nb, n_new, kvw)
    o = _decode("win", win_pt, qbd, slope_rows, scale_win, o,
                _pad_rows(new_win[:, :, :kw], PAGE), _pad_rows(new_win[:, :, kw:], PAGE), None,
                win_t, None, n_pp=win_pages, n_new=n_new, past=past, key_base=past - wb)
    shape5 = (nb, n_new, 2, N_KV_A, HEAD_DIM)
    win_all = jnp.concatenate([cache_win, kv_win.reshape(shape5)], axis=1)[:, n_new:]
    return (o.reshape(nb * n_new, d), kv_cmp.reshape(shape5), kv_sel.reshape(shape5), win_all)


def _fox_weights(b_w_in, b_q_norm, b_k_norm, b_f_bias):
    d = N_HEADS * HEAD_DIM
    w = b_w_in.astype(BF16)
    return dict(wq=w[:, :d], wk=w[:, d:2 * d], wv=w[:, 2 * d:3 * d], wf=_pad_cols(w[:, 3 * d:], LANES),
                qn=_tile_heads(b_q_norm, N_HEADS), kn=_tile_heads(b_k_norm, N_HEADS),
                fb=_pad_cols(b_f_bias[None, :].astype(F32), LANES))


def _fox_prompt(xp, g_mix, wts, tm, tq, tk):
    bd = _seg_matrix()
    q, kv, lf, kv_b = _proj_b(xp, g_mix, wts["wq"], wts["wk"], wts["wv"], wts["wf"], bd,
                              wts["qn"], wts["kn"], wts["fb"], tm)
    logf = lf[:, :N_HEADS]
    ct2 = _cumsum_t(logf.T, min(512, xp.shape[0]))
    o = _flash("fox", q, kv_b, (ct2,), tq, tk)
    return o, kv, logf


def _fox_sample(xs, g_mix, wts, cache_kv, cache_logf, page_table, tm):
    nb, n_new, d = xs.shape
    n_pages = page_table.shape[1]
    past = n_pages * PAGE
    bd = _seg_matrix()
    q, kv, lf, _ = _proj_b(xs.reshape(nb * n_new, d), g_mix, wts["wq"], wts["wk"], wts["wv"], wts["wf"], bd,
                           wts["qn"], wts["kn"], wts["fb"], tm)
    logf = lf[:, :N_HEADS]
    n_rows = n_new * N_HEADS
    qbd = _block_diag_q(q.reshape(nb, n_new, N_HEADS, HEAD_DIM), N_HEADS)
    lfn = jnp.transpose(logf.reshape(nb, n_new, N_HEADS), (0, 2, 1))
    lfn = jnp.tile(jnp.pad(lfn, ((0, 0), (0, 0), (0, PAGE - n_new))), (1, n_new, 1))
    ones = jnp.ones((nb, n_rows, 1), F32)
    zeros = jnp.zeros((nb, n_rows, HEAD_DIM), F32)
    new_kv = kv.reshape(nb, n_new, 2 * d)
    o = _decode("fox", page_table, qbd, jnp.zeros((n_rows, 1), F32), ones, zeros,
                _pad_rows(new_kv[:, :, :d], PAGE), _pad_rows(new_kv[:, :, d:], PAGE), lfn,
                _pages_t(cache_kv), jnp.transpose(cache_logf, (0, 2, 1)),
                n_pp=min(8, n_pages), n_new=n_new, past=past, key_base=0)
    return (o.reshape(nb * n_new, d), kv.reshape(nb, n_new, 2, N_HEADS, HEAD_DIM),
            logf.reshape(nb, n_new, N_HEADS))


def kernel(x_prompt, x_sample, cache_a_cmp_kv, cache_a_sel_kv, cache_a_win_kv, cache_b_kv, cache_b_logf,
           page_table, norm_mix, norm_ffn, a_w_in, a_q_norm, a_k_norm, a_gate_b, a_cmp_pos, a_cmp_w1,
           a_cmp_w2, a_w_out, b_w_in, b_q_norm, b_k_norm, b_f_bias, b_w_out, f_w_gate, f_w_up, f_w_down,
           m_router, m_w_gate, m_w_up, m_w_down):
    batch, seq, d = x_prompt.shape
    nb, n_new, _ = x_sample.shape
    assert batch == 1
    ts = nb * n_new
    tm_p = min(512, seq)
    tq_cmp = min(256, seq)
    tq = min(512, seq)
    tk = min(512, seq)
    tm_f = min(512, seq)
    dff = f_w_gate.shape[2]
    tf = dff // 2 if (dff // 2) % LANES == 0 else dff
    xp = x_prompt.reshape(seq, d)
    xs = x_sample.reshape(ts, d)
    gm = norm_mix.astype(F32)[:, None, :]
    gf = norm_ffn.astype(F32)[:, None, :]

    wa = _nsa_weights(a_w_in[0], a_q_norm[0], a_k_norm[0], a_gate_b[0], a_cmp_pos[0], a_cmp_w1[0],
                      a_cmp_w2[0], d)
    op, cmp_p, sel_p, win_p = _nsa_prompt(xp, gm[0], wa, tm_p, tq_cmp, tq, tk)
    os_, cmp_s, sel_s, win_s = _nsa_sample(x_sample, gm[0], wa, cache_a_cmp_kv[0], cache_a_sel_kv[0],
                                           cache_a_win_kv[0], page_table, ts)
    w_out = a_w_out[0].astype(BF16)
    fg, fu, fd = f_w_gate[0].astype(BF16), f_w_up[0].astype(BF16), f_w_down[0].astype(BF16)
    xp = _ffn(xp, op, w_out, gf[0], fg, fu, fd, tm_f, tf)
    xs = _ffn(xs, os_, w_out, gf[0], fg, fu, fd, ts, tf)

    wb = _fox_weights(b_w_in[0], b_q_norm[0], b_k_norm[0], b_f_bias[0])
    op, kv_p, lf_p = _fox_prompt(xp, gm[1], wb, tm_p, tq, tk)
    os_, kv_s, lf_s = _fox_sample(xs.reshape(nb, n_new, d), gm[1], wb, cache_b_kv[0], cache_b_logf[0],
                                  page_table, ts)
    w_out = b_w_out[0].astype(BF16)
    router = _pad_cols(m_router[0].astype(BF16), LANES)
    mg, mu, md = m_w_gate[0].astype(BF16), m_w_up[0].astype(BF16), m_w_down[0].astype(BF16)
    xp = _moe(xp, op, w_out, gf[1], router, mg, mu, md, tm_f, tf)
    xs = _moe(xs, os_, w_out, gf[1], router, mg, mu, md, ts, tf)

    wlen = min(WINDOW, seq)
    kv5 = lambda a, rows: a.reshape(1, 1, rows, 2, N_KV_A, HEAD_DIM)
    return (xp.reshape(batch, seq, d), xs.reshape(nb, n_new, d),
            kv5(cmp_p, seq), cmp_s[None],
            kv5(sel_p, seq), sel_s[None],
            kv5(win_p[seq - wlen:], wlen), win_s[None],
            kv_p.reshape(1, 1, seq, 2, N_HEADS, HEAD_DIM), kv_s[None],
            lf_p.reshape(1, 1, seq, N_HEADS), lf_s[None])
```

```python
import functools

import numpy as np
import jax
import jax.numpy as jnp
from jax import lax
from jax.experimental import pallas as pl
from jax.experimental.pallas import tpu as pltpu

F32 = jnp.float32
BF16 = jnp.bfloat16

HEAD_DIM = 64
N_HEADS = 16
N_KV_A = 4
GROUP_A = N_HEADS // N_KV_A
CMP_BLOCK = 32
CMP_STRIDE = 16
CMP_HIDDEN = 256
SEL_BLOCK = 64
TOP_N = 16
WINDOW = 512
N_EXPERTS = 8
PAGE = 128
RMS_EPS = 1e-6
FORCE_SCORE = 1e4
ATTN_SCALE = HEAD_DIM ** -0.5
LOG2E = 1.4426950408889634
LANES = 128
SEG_CHUNK = 256
ROW_CHUNK = 64

NEG = -1e30
PICKED = -3e38
VMEM_LIMIT = 56 * 1024 * 1024


def _alibi_slopes():
    return (2.0 ** (-8.0 * np.arange(1, N_HEADS + 1) / N_HEADS)).astype(np.float32)


def _slopes_log2():
    return tuple(float(np.float32(s) * np.float32(LOG2E)) for s in _alibi_slopes())


def _dot(a, b):
    return jnp.dot(a, b, preferred_element_type=F32)


def _dot_nt(a, b):
    return lax.dot_general(a, b, (((1,), (1,)), ((), ())), preferred_element_type=F32)


def _dot3(x, m):
    x1 = x.astype(BF16)
    r1 = x - x1.astype(F32)
    x2 = r1.astype(BF16)
    x3 = (r1 - x2.astype(F32)).astype(BF16)
    return _dot(x1, m) + _dot(x2, m) + _dot(x3, m)


def _rms(x, g):
    y = x * lax.rsqrt(jnp.mean(x * x, axis=-1, keepdims=True) + RMS_EPS)
    return y * g


def _seg_norm(y, bd, g):
    outs = []
    for c in range(y.shape[1] // SEG_CHUNK):
        yc = y[:, c * SEG_CHUNK:(c + 1) * SEG_CHUNK]
        ms = _dot3(yc * yc, bd)
        outs.append(yc * lax.rsqrt(ms + RMS_EPS))
    out = outs[0] if len(outs) == 1 else jnp.concatenate(outs, axis=1)
    return out * g


def _silu(x):
    return x * jax.nn.sigmoid(x)


def _cparams(sem):
    return pltpu.CompilerParams(dimension_semantics=sem, vmem_limit_bytes=VMEM_LIMIT)


def _seg_matrix():
    i = np.arange(SEG_CHUNK)
    m = (i[:, None] // HEAD_DIM == i[None, :] // HEAD_DIM).astype(np.float32) / HEAD_DIM
    return jnp.asarray(m, BF16)


def _proj_a_kernel(x_ref, g_ref, wq_ref, wkv_ref, wg_ref, bd_ref, qn_ref, kn1_ref, kn2_ref, gb_ref,
                   q_out, cmp_out, sel_out, win_out, gate_out, selb_out, winb_out):
    h = _rms(x_ref[...], g_ref[...]).astype(BF16)
    bd = bd_ref[...]
    q_out[...] = _seg_norm(_dot(h, wq_ref[...]), bd, qn_ref[...])
    kv = _dot(h, wkv_ref[...])
    kvw = 2 * N_KV_A * HEAD_DIM
    kw = N_KV_A * HEAD_DIM
    cmp_out[...] = kv[:, :kvw]
    for base, kn_ref, out, outb in ((kvw, kn1_ref, sel_out, selb_out), (2 * kvw, kn2_ref, win_out, winb_out)):
        kn = _seg_norm(kv[:, base:base + kw], bd, kn_ref[...])
        v = kv[:, base + kw:base + kvw]
        out[:, :kw] = kn
        out[:, kw:] = v
        outb[:, :kw] = kn.astype(BF16)
        outb[:, kw:] = v.astype(BF16)
    gate_out[...] = jax.nn.sigmoid(_dot(h, wg_ref[...]) + gb_ref[...])


def _proj_a(x, g, wq, wkv, wg, bd, qn, kn1, kn2, gb, tm):
    t, d = x.shape
    kvw = 2 * N_KV_A * HEAD_DIM
    const = lambda shape: pl.BlockSpec(shape, lambda i: (0, 0))
    row = lambda w: pl.BlockSpec((tm, w), lambda i: (i, 0))
    return pl.pallas_call(
        _proj_a_kernel,
        grid=(t // tm,),
        in_specs=[row(d), const((1, d)), const(wq.shape), const(wkv.shape), const(wg.shape),
                  const(bd.shape), const(qn.shape), const(kn1.shape), const(kn2.shape), const(gb.shape)],
        out_specs=[row(d), row(kvw), row(kvw), row(kvw), row(LANES), row(kvw), row(kvw)],
        out_shape=[jax.ShapeDtypeStruct((t, d), F32)] + [jax.ShapeDtypeStruct((t, kvw), F32)] * 3
                  + [jax.ShapeDtypeStruct((t, LANES), F32)] + [jax.ShapeDtypeStruct((t, kvw), BF16)] * 2,
        compiler_params=_cparams(("parallel",)),
        name="proj_a",
    )(x, g, wq, wkv, wg, bd, qn, kn1, kn2, gb)


def _proj_b_kernel(x_ref, g_ref, wq_ref, wk_ref, wv_ref, wf_ref, bd_ref, qn_ref, kn_ref, fb_ref,
                   q_out, kv_out, lf_out, kvb_out):
    h = _rms(x_ref[...], g_ref[...]).astype(BF16)
    bd = bd_ref[...]
    d = q_out.shape[1]
    q_out[...] = _seg_norm(_dot(h, wq_ref[...]), bd, qn_ref[...])
    kn = _seg_norm(_dot(h, wk_ref[...]), bd, kn_ref[...])
    v = _dot(h, wv_ref[...])
    kv_out[:, :d] = kn
    kv_out[:, d:] = v
    kvb_out[:, :d] = kn.astype(BF16)
    kvb_out[:, d:] = v.astype(BF16)
    z = _dot(h, wf_ref[...]) + fb_ref[...]
    lf_out[...] = jnp.minimum(z, 0.0) - jnp.log1p(jnp.exp(-jnp.abs(z)))


def _proj_b(x, g, wq, wk, wv, wf, bd, qn, kn, fb, tm):
    t, d = x.shape
    const = lambda shape: pl.BlockSpec(shape, lambda i: (0, 0))
    row = lambda w: pl.BlockSpec((tm, w), lambda i: (i, 0))
    return pl.pallas_call(
        _proj_b_kernel,
        grid=(t // tm,),
        in_specs=[row(d), const((1, d)), const(wq.shape), const(wk.shape), const(wv.shape), const(wf.shape),
                  const(bd.shape), const(qn.shape), const(kn.shape), const(fb.shape)],
        out_specs=[row(d), row(2 * d), row(LANES), row(2 * d)],
        out_shape=[jax.ShapeDtypeStruct((t, d), F32), jax.ShapeDtypeStruct((t, 2 * d), F32),
                   jax.ShapeDtypeStruct((t, LANES), F32), jax.ShapeDtypeStruct((t, 2 * d), BF16)],
        compiler_params=_cparams(("parallel",)),
        name="proj_b",
    )(x, g, wq, wk, wv, wf, bd, qn, kn, fb)


def _ffn_kernel(x_ref, a_ref, wo_ref, g_ref, wg_ref, wu_ref, wd_ref, o_ref, xm_ref, xn_ref, acc_ref):
    j = pl.program_id(1)

    @pl.when(j == 0)
    def _():
        xm = x_ref[...] + _dot(a_ref[...].astype(BF16), wo_ref[...])
        xm_ref[...] = xm
        xn_ref[...] = _rms(xm, g_ref[...]).astype(BF16)
        acc_ref[...] = jnp.zeros_like(acc_ref)

    xn = xn_ref[...]
    hmid = _silu(_dot(xn, wg_ref[...])) * _dot(xn, wu_ref[...])
    acc_ref[...] += _dot(hmid.astype(BF16), wd_ref[...])

    @pl.when(j == pl.num_programs(1) - 1)
    def _():
        o_ref[...] = xm_ref[...] + acc_ref[...]


def _ffn(x, a, wo, g, wg, wu, wd, tm, tf):
    t, d = x.shape
    dff = wg.shape[1]
    row = pl.BlockSpec((tm, d), lambda i, j: (i, 0))
    return pl.pallas_call(
        _ffn_kernel,
        grid=(t // tm, dff // tf),
        in_specs=[row, row, pl.BlockSpec(wo.shape, lambda i, j: (0, 0)),
                  pl.BlockSpec((1, d), lambda i, j: (0, 0)),
                  pl.BlockSpec((d, tf), lambda i, j: (0, j)),
                  pl.BlockSpec((d, tf), lambda i, j: (0, j)),
                  pl.BlockSpec((tf, d), lambda i, j: (j, 0))],
        out_specs=row,
        out_shape=jax.ShapeDtypeStruct((t, d), F32),
        scratch_shapes=[pltpu.VMEM((tm, d), F32), pltpu.VMEM((tm, d), BF16), pltpu.VMEM((tm, d), F32)],
        compiler_params=_cparams(("parallel", "arbitrary")),
        name="ffn",
    )(x, a, wo, g, wg, wu, wd)


def _moe_kernel(x_ref, a_ref, wo_ref, g_ref, r_ref, wg_ref, wu_ref, wd_ref, o_ref,
                xm_ref, xn_ref, gate_ref, acc_ref):
    e = pl.program_id(1)
    j = pl.program_id(2)
    last_j = pl.num_programs(2) - 1

    @pl.when((e == 0) & (j == 0))
    def _():
        xm = x_ref[...] + _dot(a_ref[...].astype(BF16), wo_ref[...])
        xm_ref[...] = xm
        xn = _rms(xm, g_ref[...]).astype(BF16)
        xn_ref[...] = xn
        acc_ref[...] = jnp.zeros_like(acc_ref)
        logits = _dot(xn, r_ref[...])
        col = lax.broadcasted_iota(jnp.int32, logits.shape, 1).astype(F32)
        real = col < N_EXPERTS
        logits = jnp.where(real, logits, NEG)
        mx = jnp.max(logits, axis=-1, keepdims=True)
        pe = jnp.where(real, jnp.exp(logits - mx), 0.0)
        probs = pe / jnp.sum(pe, axis=-1, keepdims=True)
        work = jnp.where(real, probs, PICKED)
        picked = jnp.zeros(logits.shape, jnp.bool_)
        for _ in range(2):
            top = jnp.max(work, axis=-1, keepdims=True)
            idx = jnp.min(jnp.where(work == top, col, float(LANES)), axis=-1, keepdims=True)
            hit = col == idx
            picked = picked | hit
            work = jnp.where(hit, PICKED, work)
        sel = jnp.where(picked, probs, 0.0)
        gate_ref[...] = sel / jnp.sum(sel, axis=-1, keepdims=True)

    xn = xn_ref[...]
    hmid = _silu(_dot(xn, wg_ref[...])) * _dot(xn, wu_ref[...])
    gate = gate_ref[...]
    col = lax.broadcasted_iota(jnp.int32, gate.shape, 1)
    gcol = jnp.sum(jnp.where(col == e, gate, 0.0), axis=-1, keepdims=True)
    acc_ref[...] += gcol * _dot(hmid.astype(BF16), wd_ref[...])

    @pl.when((e == pl.num_programs(1) - 1) & (j == last_j))
    def _():
        o_ref[...] = xm_ref[...] + acc_ref[...]


def _moe(x, a, wo, g, router, wg, wu, wd, tm, tf):
    t, d = x.shape
    ne, _, dff = wg.shape
    row = pl.BlockSpec((tm, d), lambda i, e, j: (i, 0))
    return pl.pallas_call(
        _moe_kernel,
        grid=(t // tm, ne, dff // tf),
        in_specs=[row, row, pl.BlockSpec(wo.shape, lambda i, e, j: (0, 0)),
                  pl.BlockSpec((1, d), lambda i, e, j: (0, 0)),
                  pl.BlockSpec(router.shape, lambda i, e, j: (0, 0)),
                  pl.BlockSpec((None, d, tf), lambda i, e, j: (e, 0, j)),
                  pl.BlockSpec((None, d, tf), lambda i, e, j: (e, 0, j)),
                  pl.BlockSpec((None, tf, d), lambda i, e, j: (e, j, 0))],
        out_specs=row,
        out_shape=jax.ShapeDtypeStruct((t, d), F32),
        scratch_shapes=[pltpu.VMEM((tm, d), F32), pltpu.VMEM((tm, d), BF16), pltpu.VMEM((tm, LANES), F32),
                        pltpu.VMEM((tm, d), F32)],
        compiler_params=_cparams(("parallel", "arbitrary", "arbitrary")),
        name="moe",
    )(x, a, wo, g, router, wg, wu, wd)


ROWS_PER_PAGE = PAGE // CMP_STRIDE


def _compress_kernel(pt_ref, x_hbm, pos_ref, w1_ref, w2_ref, kn_ref, o_ref, xbuf, sem, *,
                     n_pages, n_parts, n_kvg):
    b = pl.program_id(0)
    part = pl.program_id(1)
    step = b * n_parts + part
    n_steps = pl.num_programs(0) * n_parts
    slot = step % 2
    pp = n_pages // n_parts

    def page_copy(bb, prt, sl, p):
        page = jnp.minimum(prt * pp + p, n_pages - 1)
        return pltpu.make_async_copy(x_hbm.at[pt_ref[bb, page]], xbuf.at[sl, p], sem.at[sl])

    def fetch(bb, prt, sl):
        def body(p, carry):
            page_copy(bb, prt, sl, p).start()
            return carry
        lax.fori_loop(0, pp + 1, body, 0)

    @pl.when(step == 0)
    def _():
        fetch(b, part, slot)

    @pl.when(step + 1 < n_steps)
    def _():
        nxt = step + 1
        fetch(nxt // n_parts, nxt % n_parts, 1 - slot)

    def wait_body(p, carry):
        page_copy(b, part, slot, p).wait()
        return carry
    lax.fori_loop(0, pp + 1, wait_body, 0)

    n_in = (pp + 1) * ROWS_PER_PAGE
    n_out = pp * ROWS_PER_PAGE
    for c in range(n_kvg):
        kv = c // (n_kvg // 2)
        x = xbuf[slot, :, c].reshape(n_in, xbuf.shape[-1])
        z0 = _dot((x + pos_ref[kv, 0]).astype(BF16), w1_ref[kv, 0])
        z1 = _dot((x + pos_ref[kv, 1]).astype(BF16), w1_ref[kv, 1])
        hid = z0[:n_out] + pltpu.roll(z1, n_in - 1, 0)[:n_out]
        y = _dot(_silu(hid).astype(BF16), w2_ref[kv])
        o_ref[c] = _rms(y, kn_ref[...]) if kv == 0 else y


def _compress(pt, xr, posr, w1r, w2, kn):
    nb, n_pages = pt.shape
    n_kvg = xr.shape[1]
    n_parts = 2 if n_pages % 2 == 0 else 1
    pp = n_pages // n_parts
    n_out = pp * ROWS_PER_PAGE
    width = xr.shape[3]
    const = lambda a: pl.BlockSpec(a.shape, lambda b, s, pt: (0,) * a.ndim)
    grid_spec = pltpu.PrefetchScalarGridSpec(
        num_scalar_prefetch=1,
        grid=(nb, n_parts),
        in_specs=[pl.BlockSpec(memory_space=pl.ANY), const(posr), const(w1r), const(w2), const(kn)],
        out_specs=pl.BlockSpec((None, n_kvg, n_out, HEAD_DIM), lambda b, s, pt: (b, 0, s, 0)),
        scratch_shapes=[pltpu.VMEM((2, pp + 1, n_kvg, ROWS_PER_PAGE, width), F32),
                        pltpu.SemaphoreType.DMA((2,))],
    )
    return pl.pallas_call(
        functools.partial(_compress_kernel, n_pages=n_pages, n_parts=n_parts, n_kvg=n_kvg),
        grid_spec=grid_spec,
        out_shape=jax.ShapeDtypeStruct((nb, n_kvg, n_pages * ROWS_PER_PAGE, HEAD_DIM), F32),
        compiler_params=_cparams(("arbitrary", "arbitrary")),
        name="compress",
    )(pt, xr, posr, w1r, w2, kn)


def _cmp_topk_kernel(q_ref, pos_ref, kc_ref, vc_ref, ov_ref, gate_ref, o_ref, sb_ref, *,
                     tq, n_cmp, n_sel, n_top, slopes2):
    nc = kc_ref.shape[0]
    ns = ov_ref.shape[1]
    q = q_ref[...]
    pos = pos_ref[...]
    gates = gate_ref[...]
    n_idx = lax.broadcasted_iota(jnp.int32, (1, nc), 1)
    dcmp = pos - (n_idx * CMP_STRIDE + (CMP_BLOCK - 1))
    mask = (dcmp >= 0) & (n_idx < n_cmp)
    rel = (n_idx * CMP_STRIDE + (CMP_BLOCK - 1) - pos[0:1, :]).astype(F32)
    blk = lax.broadcasted_iota(jnp.int32, (1, ns), 1)
    blkf = blk.astype(F32)
    cur = pos // SEL_BLOCK
    forced = (blk == 0) | (blk == cur) | (blk == cur - 1)
    valid = (blk * SEL_BLOCK <= pos) & (blk < n_sel)
    ov = ov_ref[...]
    for g in range(N_KV_A):
        heads = range(g * GROUP_A, (g + 1) * GROUP_A)
        qg = (jnp.concatenate([q[:, h * HEAD_DIM:(h + 1) * HEAD_DIM] for h in heads], axis=0)
              * (ATTN_SCALE * LOG2E)).astype(BF16)
        kg = kc_ref[:, g * HEAD_DIM:(g + 1) * HEAD_DIM].astype(BF16)
        vg = vc_ref[:, g * HEAD_DIM:(g + 1) * HEAD_DIM].astype(BF16)
        s = _dot_nt(qg, kg)
        psum = jnp.zeros((tq, nc), F32)
        for r, h in enumerate(heads):
            sr = jnp.where(mask, s[r * tq:(r + 1) * tq] + slopes2[h] * rel, NEG)
            mx = jnp.max(sr, axis=-1, keepdims=True)
            p = jnp.where(mask, jnp.exp2(sr - mx), 0.0)
            p = p * (1.0 / jnp.maximum(jnp.sum(p, axis=-1, keepdims=True), 1e-30))
            o = _dot(p.astype(BF16), vg)
            o_ref[:, h * HEAD_DIM:(h + 1) * HEAD_DIM] = gates[:, h:h + 1] * o
            psum = psum + p
        imp = _dot(psum.astype(BF16), ov)
        imp = jnp.where(forced, FORCE_SCORE, imp)
        work = jnp.where(valid, imp, NEG)
        work = jnp.where(blk < n_sel, work, PICKED)
        picked = jnp.zeros((tq, ns), jnp.bool_)
        for _ in range(n_top):
            top = jnp.max(work, axis=-1, keepdims=True)
            idx = jnp.min(jnp.where(work == top, blkf, float(ns)), axis=-1, keepdims=True)
            hit = blkf == idx
            picked = picked | hit
            work = jnp.where(hit, PICKED, work)
        sb_ref[:, g * ns:(g + 1) * ns] = jnp.where(picked, 0.0, NEG).astype(BF16)


def _cmp_topk(q, pos, kc, vc, ov, gates, tq, tiles_per_seq, n_cmp, n_sel):
    t, d = q.shape
    nc = kc.shape[1]
    ns = ov.shape[1]
    kvw = N_KV_A * HEAD_DIM
    row = lambda w: pl.BlockSpec((tq, w), lambda i: (i, 0))
    seq = pl.BlockSpec((None, nc, kvw), lambda i: (i // tiles_per_seq, 0, 0))
    return pl.pallas_call(
        functools.partial(_cmp_topk_kernel, tq=tq, n_cmp=n_cmp, n_sel=n_sel,
                          n_top=min(TOP_N, n_sel), slopes2=_slopes_log2()),
        grid=(t // tq,),
        in_specs=[row(d), row(1), seq, seq, pl.BlockSpec(ov.shape, lambda i: (0, 0)), row(LANES)],
        out_specs=[row(d), row(N_KV_A * ns)],
        out_shape=[jax.ShapeDtypeStruct((t, d), F32), jax.ShapeDtypeStruct((t, N_KV_A * ns), BF16)],
        compiler_params=_cparams(("parallel",)),
        name="cmp_topk",
    )(q, pos, kc, vc, ov, gates)


def _flash_kernel(qi_ref, ki_ref, first_ref, last_ref, *refs, kind, kv_heads, tq, tk, slopes2, gate_off):
    qs_ref, m_ref, l_ref, acc_ref, p_ref = refs[-5:]
    if kind == "fox":
        q_ref, kv_ref, ct_ref, o_ref = refs[:-5]
    elif kind == "sel":
        q_ref, kv_ref, sb_ref, prev_ref, gate_ref, o_ref = refs[:-5]
    else:
        q_ref, kv_ref, prev_ref, gate_ref, o_ref = refs[:-5]
    rep = N_HEADS // kv_heads
    v_off = kv_heads * HEAD_DIM
    s_id = pl.program_id(0)
    kt = ki_ref[s_id]
    q0 = qi_ref[s_id] * tq
    low = lax.broadcasted_iota(jnp.int32, (1, LANES), 1) < HEAD_DIM

    def kv_half(h):
        return (h // rep) % 2

    @pl.when(first_ref[s_id] == 1)
    def _():
        m_ref[...] = jnp.full(m_ref.shape, NEG, F32)
        l_ref[...] = jnp.zeros_like(l_ref)
        acc_ref[...] = jnp.zeros_like(acc_ref)
        for j in range(N_HEADS // 2):
            q2 = q_ref[:, j * LANES:(j + 1) * LANES] * (ATTN_SCALE * LOG2E)
            for e in range(2):
                h = 2 * j + e
                mine = jnp.where(low if e == 0 else jnp.logical_not(low), q2, 0.0)
                if kv_half(h) != e:
                    mine = pltpu.roll(mine, HEAD_DIM, 1)
                qs_ref[h * tq:(h + 1) * tq, :] = mine.astype(BF16)

    def step(masked):
        kpos = kt * tk + lax.broadcasted_iota(jnp.int32, (1, tk), 1)
        if kind != "fox":
            rel = (kpos - q0).astype(F32)
        if kind == "sel":
            sbt = sb_ref[...]
            srow = lax.broadcasted_iota(jnp.int32, (sbt.shape[1], 1), 0)
            blk_in_tile = lax.broadcasted_iota(jnp.int32, (1, tk), 1) // SEL_BLOCK
            per_tile = tk // SEL_BLOCK
        for g in range(kv_heads):
            pair = g // 2
            k2 = kv_ref[:, pair * LANES:(pair + 1) * LANES]
            v2 = kv_ref[:, v_off + pair * LANES:v_off + (pair + 1) * LANES]
            if kind == "fox":
                brow = -ct_ref[g:g + 1, :]
            if kind == "sel":
                selb = _dot(sbt, (srow == g * per_tile + blk_in_tile).astype(BF16))
            for r in range(rep):
                h = g * rep + r
                hr = slice(h * tq, (h + 1) * tq)
                s = _dot_nt(qs_ref[hr, :], k2)
                if kind != "fox":
                    brow = slopes2[h] * rel
                for c in range(tq // ROW_CHUNK):
                    cr = slice(c * ROW_CHUNK, (c + 1) * ROW_CHUNK)
                    hc = slice(h * tq + c * ROW_CHUNK, h * tq + (c + 1) * ROW_CHUNK)
                    sc = s[cr] + brow
                    if kind == "sel":
                        sc = sc + selb[cr]
                    if masked:
                        tpos = q0 + c * ROW_CHUNK + lax.broadcasted_iota(jnp.int32, (ROW_CHUNK, 1), 0)
                        dist = tpos - kpos
                        keep = (dist >= 0) & (dist < WINDOW) if kind == "win" else dist >= 0
                        sc = jnp.where(keep, sc, NEG)
                    m_prev = m_ref[hc, :]
                    m_new = jnp.maximum(m_prev, jnp.max(sc, axis=-1, keepdims=True))
                    alpha = jnp.exp2(m_prev - m_new)
                    pc = jnp.exp2(sc - jnp.tile(m_new, (1, tk // LANES)))
                    l_ref[hc, :] = alpha * l_ref[hc, :] + jnp.sum(pc, axis=-1, keepdims=True)
                    acc_ref[hc, :] = alpha * acc_ref[hc, :]
                    m_ref[hc, :] = m_new
                    p_ref[h % 2, cr, :] = pc.astype(BF16)
                acc_ref[hr, :] += _dot(p_ref[h % 2], v2)

    if kind == "win":
        step(True)
    else:
        crosses = (kt + 1) * tk - 1 > q0
        pl.when(crosses)(lambda: step(True))
        pl.when(jnp.logical_not(crosses))(lambda: step(False))

    @pl.when(last_ref[s_id] == 1)
    def _():
        for j in range(N_HEADS // 2):
            halves = []
            for e in range(2):
                h = 2 * j + e
                hr = slice(h * tq, (h + 1) * tq)
                o = acc_ref[hr, :] / l_ref[hr, :]
                if kv_half(h) != e:
                    o = pltpu.roll(o, HEAD_DIM, 1)
                halves.append(o)
            o2 = jnp.where(low, halves[0], halves[1])
            cols = slice(j * LANES, (j + 1) * LANES)
            if kind == "fox":
                o_ref[:, cols] = o2
            else:
                c0 = gate_off + 2 * j
                g2 = jnp.where(low, gate_ref[:, c0:c0 + 1], gate_ref[:, c0 + 1:c0 + 2])
                o_ref[:, cols] = prev_ref[:, cols] + g2 * o2


def _flash(kind, q, kv, extras, tq, tk):
    t, d = q.shape
    kv_heads = kv.shape[1] // (2 * HEAD_DIM)
    nq = t // tq
    qi, ki, first, last = [], [], [], []
    for i in range(nq):
        hi = (i * tq + tq - 1) // tk
        lo = max(i * tq - (WINDOW - 1), 0) // tk if kind == "win" else 0
        for kt in range(lo, hi + 1):
            qi.append(i)
            ki.append(kt)
            first.append(int(kt == lo))
            last.append(int(kt == hi))
    tables = [jnp.asarray(np.asarray(a, np.int32)) for a in (qi, ki, first, last)]

    qrow = lambda w: pl.BlockSpec((tq, w), lambda s, qi, ki, fi, la: (qi[s], 0))
    in_specs = [qrow(d), pl.BlockSpec((tk, kv.shape[1]), lambda s, qi, ki, fi, la: (ki[s], 0))]
    if kind == "fox":
        (ct2,) = extras
        in_specs += [pl.BlockSpec((ct2.shape[0], tk), lambda s, qi, ki, fi, la: (0, ki[s]))]
        gate_off = 0
    elif kind == "sel":
        sbt, prev, gates = extras
        in_specs += [pl.BlockSpec((None, tq, sbt.shape[2]), lambda s, qi, ki, fi, la: (ki[s], qi[s], 0)),
                     qrow(d), qrow(LANES)]
        gate_off = N_HEADS
    else:
        prev, gates = extras
        in_specs += [qrow(d), qrow(LANES)]
        gate_off = 2 * N_HEADS
    grid_spec = pltpu.PrefetchScalarGridSpec(
        num_scalar_prefetch=4, grid=(len(qi),), in_specs=in_specs, out_specs=qrow(d),
        scratch_shapes=[pltpu.VMEM((N_HEADS * tq, LANES), BF16), pltpu.VMEM((N_HEADS * tq, LANES), F32),
                        pltpu.VMEM((N_HEADS * tq, LANES), F32), pltpu.VMEM((N_HEADS * tq, LANES), F32),
                        pltpu.VMEM((2, tq, tk), BF16)])
    return pl.pallas_call(
        functools.partial(_flash_kernel, kind=kind, kv_heads=kv_heads, tq=tq, tk=tk,
                          slopes2=_slopes_log2(), gate_off=gate_off),
        grid_spec=grid_spec,
        out_shape=jax.ShapeDtypeStruct((t, d), F32),
        compiler_params=_cparams(("arbitrary",)),
        name="flash_" + kind,
    )(*tables, q, kv, *extras)


def _cumsum_kernel(x_ref, o_ref, carry_ref):
    @pl.when(pl.program_id(0) == 0)
    def _():
        carry_ref[...] = jnp.zeros_like(carry_ref)

    x = x_ref[...]
    tm = x.shape[1]
    upper = (lax.broadcasted_iota(jnp.int32, (tm, tm), 0)
             <= lax.broadcasted_iota(jnp.int32, (tm, tm), 1)).astype(BF16)
    c = _dot3(x, upper) + carry_ref[:, 0:1]
    o_ref[...] = c * LOG2E
    carry_ref[...] = jnp.broadcast_to(c[:, tm - 1:tm], carry_ref.shape)


def _cumsum_t(xt, tm):
    nh, t = xt.shape
    blk = pl.BlockSpec((nh, tm), lambda i: (0, i))
    return pl.pallas_call(
        _cumsum_kernel,
        grid=(t // tm,),
        in_specs=[blk],
        out_specs=blk,
        out_shape=jax.ShapeDtypeStruct((nh, t), F32),
        scratch_shapes=[pltpu.VMEM((nh, LANES), F32)],
        compiler_params=_cparams(("arbitrary",)),
        name="cumsum",
    )(xt)


def _decode_kernel(*refs, kind, n_pp, n_pages, n_new, past, key_base):
    pt_ref = refs[0]
    qbd_ref, slope_ref, scale_ref, prev_ref, knew_ref, vnew_ref = refs[1:7]
    k = 7
    if kind == "fox":
        lfn_ref = refs[k]
        k += 1
    elif kind == "sel":
        sb_ref = refs[k]
        k += 1
    page_refs = refs[k:k + n_pp]
    k += n_pp
    if kind == "fox":
        lf_refs = refs[k:k + n_pp]
        k += n_pp
    o_ref = refs[k]
    m_ref, l_ref, acc_ref = refs[k + 1:k + 4]
    if kind == "fox":
        carry_ref = refs[k + 4]
    del pt_ref

    s_id = pl.program_id(1)
    n_rows, c_dim = qbd_ref.shape
    kv_heads = c_dim // HEAD_DIM
    row = lax.broadcasted_iota(jnp.int32, (n_rows, 1), 0)
    t_row = row // N_HEADS
    h_row = row % N_HEADS
    lane = lax.broadcasted_iota(jnp.int32, (1, PAGE), 1)
    qbd = qbd_ref[...]

    @pl.when(s_id == 0)
    def _():
        m_ref[...] = jnp.full(m_ref.shape, NEG, F32)
        l_ref[...] = jnp.zeros_like(l_ref)
        acc_ref[...] = jnp.zeros_like(acc_ref)
        if kind == "fox":
            carry_ref[...] = jnp.zeros_like(carry_ref)

    if kind == "fox":
        incl = (lax.broadcasted_iota(jnp.int32, (PAGE, PAGE), 0)
                <= lax.broadcasted_iota(jnp.int32, (PAGE, PAGE), 1)).astype(BF16)
        cnu = _dot3(lfn_ref[...], incl)
        cn = jnp.sum(jnp.where(lane == t_row, cnu, 0.0), axis=-1, keepdims=True)
        after = (lax.broadcasted_iota(jnp.int32, (PAGE, PAGE), 0)
                 > lax.broadcasted_iota(jnp.int32, (PAGE, PAGE), 1)).astype(BF16)
    else:
        slope = slope_ref[...]
        qpos = past + t_row

    def update(s, v_parts):
        m_prev = m_ref[...]
        m_new = jnp.maximum(m_prev, jnp.max(s, axis=-1, keepdims=True))
        alpha = jnp.exp(m_prev - m_new)
        p = jnp.exp(s - m_new)
        l_ref[...] = alpha * l_ref[...] + jnp.sum(p, axis=-1, keepdims=True)
        acc = alpha * acc_ref[...]
        for j, (v, transposed) in enumerate(v_parts):
            pj = p[:, j * PAGE:(j + 1) * PAGE].astype(BF16)
            acc = acc + (_dot_nt(pj, v) if transposed else _dot(pj, v))
        acc_ref[...] = acc
        m_ref[...] = m_new

    s_parts, b_parts, v_parts = [], [], []
    for j in range(n_pp):
        page_idx = n_pages - 1 - (s_id * n_pp + j)
        kt = page_refs[j][0].astype(BF16)
        s_parts.append(_dot(qbd, kt))
        v_parts.append((page_refs[j][1].astype(BF16), True))
        if kind == "fox":
            lf = lf_refs[j][...]
            carry = carry_ref[:, 0:1]
            suf = _dot3(lf, after) + carry
            carry_ref[...] = jnp.broadcast_to(carry + jnp.sum(lf, axis=-1, keepdims=True), carry_ref.shape)
            b_parts.append(jnp.concatenate([suf] * (n_rows // N_HEADS), axis=0) + cn)
        else:
            kbuf = page_idx * PAGE + lane
            bias = -slope * (qpos - (key_base + kbuf)).astype(F32)
            if kind == "win":
                bias = jnp.where(kbuf > t_row, bias, NEG)
            b_parts.append(bias)
    s = jnp.concatenate(s_parts, axis=1) + jnp.concatenate(b_parts, axis=1)
    if kind == "sel":
        sb = sb_ref[...]
        ns = sb.shape[1]
        kpos = jnp.concatenate(
            [(n_pages - 1 - (s_id * n_pp + j)) * PAGE + lane for j in range(n_pp)], axis=1)
        expand = (lax.broadcasted_iota(jnp.int32, (ns, 1), 0) == kpos // SEL_BLOCK).astype(BF16)
        s = s + _dot(sb, expand)
    update(s, v_parts)

    @pl.when(s_id == pl.num_programs(1) - 1)
    def _():
        s_new = _dot_nt(qbd, knew_ref[...].astype(BF16))
        ok = (lane <= t_row) & (lane < n_new)
        if kind == "fox":
            bias = cn - cnu
        else:
            bias = -slope * (t_row - lane).astype(F32)
            if kind == "sel":
                ns = sb_ref.shape[1]
                expand_new = (lax.broadcasted_iota(jnp.int32, (ns, 1), 0)
                              == (past + lane) // SEL_BLOCK).astype(BF16)
                bias = bias + _dot(sb_ref[...], expand_new)
        update(jnp.where(ok, s_new + bias, NEG), [(vnew_ref[...].astype(BF16), False)])
        o_full = acc_ref[...] / l_ref[...]
        col_head = lax.broadcasted_iota(jnp.int32, (1, c_dim), 1) // HEAD_DIM
        own = col_head == h_row // (N_HEADS // kv_heads)
        o_full = jnp.where(own, o_full, 0.0)
        o = o_full[:, 0:HEAD_DIM]
        for c in range(1, kv_heads):
            o = o + o_full[:, c * HEAD_DIM:(c + 1) * HEAD_DIM]
        o_ref[...] = prev_ref[...] + scale_ref[...] * o


def _decode(kind, pt, qbd, slope_rows, scale_rows, prev, knew, vnew, extra, pages, lf_pages,
            n_pp, n_new, past, key_base):
    nb, n_pages = pt.shape
    n_rows, c_dim = qbd.shape[1:]
    n_steps = n_pages // n_pp
    seq = lambda shape: pl.BlockSpec((None,) + shape, lambda b, s, pt: (b,) + (0,) * len(shape))

    def page_map(j):
        return lambda b, s, pt: (pt[b, n_pages - 1 - (s * n_pp + j)], 0, 0, 0)

    def lf_map(j):
        return lambda b, s, pt: (pt[b, n_pages - 1 - (s * n_pp + j)], 0, 0)

    in_specs = [seq((n_rows, c_dim)), pl.BlockSpec((n_rows, 1), lambda b, s, pt: (0, 0)),
                seq((n_rows, 1)), seq((n_rows, HEAD_DIM)), seq((PAGE, c_dim)), seq((PAGE, c_dim))]
    args = [qbd, slope_rows, scale_rows, prev, knew, vnew]
    if extra is not None:
        in_specs.append(seq(extra.shape[1:]))
        args.append(extra)
    in_specs += [pl.BlockSpec((None, 2, c_dim, PAGE), page_map(j)) for j in range(n_pp)]
    args += [pages] * n_pp
    scratch = [pltpu.VMEM((n_rows, 1), F32), pltpu.VMEM((n_rows, 1), F32), pltpu.VMEM((n_rows, c_dim), F32)]
    if kind == "fox":
        in_specs += [pl.BlockSpec((None, N_HEADS, PAGE), lf_map(j)) for j in range(n_pp)]
        args += [lf_pages] * n_pp
        scratch.append(pltpu.VMEM((N_HEADS, LANES), F32))
    grid_spec = pltpu.PrefetchScalarGridSpec(
        num_scalar_prefetch=1, grid=(nb, n_steps), in_specs=in_specs,
        out_specs=seq((n_rows, HEAD_DIM)), scratch_shapes=scratch)
    return pl.pallas_call(
        functools.partial(_decode_kernel, kind=kind, n_pp=n_pp, n_pages=n_pages, n_new=n_new,
                          past=past, key_base=key_base),
        grid_spec=grid_spec,
        out_shape=jax.ShapeDtypeStruct((nb, n_rows, HEAD_DIM), F32),
        compiler_params=_cparams(("parallel", "arbitrary")),
        name="decode_" + kind,
    )(pt, *args)


def _overlap_matrix(n_cmp_pad, n_cmp, n_sel_pad, n_sel):
    cs = np.arange(n_cmp_pad)[:, None] * CMP_STRIDE
    ss = np.arange(n_sel_pad)[None, :] * SEL_BLOCK
    ov = np.minimum(cs + CMP_BLOCK, ss + SEL_BLOCK) - np.maximum(cs, ss)
    ov = np.clip(ov, 0, None) / CMP_BLOCK
    ov = ov * (np.arange(n_cmp_pad)[:, None] < n_cmp) * (np.arange(n_sel_pad)[None, :] < n_sel)
    return jnp.asarray(ov.astype(np.float32), BF16)


def _tile_heads(v, n):
    return jnp.tile(v.astype(F32), n)[None, :]


def _pad_cols(w, width):
    return jnp.pad(w, ((0, 0), (0, width - w.shape[1])))


def _block_diag_q(q, kv_heads):
    nb, t = q.shape[:2]
    own = (np.arange(N_HEADS)[:, None] // (N_HEADS // kv_heads) == np.arange(kv_heads)[None, :])
    qb = q[:, :, :, None, :] * jnp.asarray(own, F32)[None, None, :, :, None]
    return (qb * ATTN_SCALE).reshape(nb, t * N_HEADS, kv_heads * HEAD_DIM).astype(BF16)


def _pages_t(cache):
    pool, rows, two, heads, hd = cache.shape
    return jnp.transpose(cache, (0, 2, 3, 4, 1)).reshape(pool, two, heads * hd, rows)


def _pad_rows(x, rows):
    return jnp.pad(x, ((0, 0), (0, rows - x.shape[1]), (0, 0)))


def _nsa_weights(a_w_in, a_q_norm, a_k_norm, a_gate_b, a_cmp_pos, a_cmp_w1, a_cmp_w2, d):
    qc = N_HEADS * HEAD_DIM
    kvw = 2 * N_KV_A * HEAD_DIM
    w = a_w_in.astype(BF16)
    half = CMP_STRIDE * HEAD_DIM
    return dict(
        wq=w[:, :qc], wkv=w[:, qc:qc + 3 * kvw], wg=_pad_cols(w[:, qc + 3 * kvw:], LANES),
        qn=_tile_heads(a_q_norm, N_HEADS), kn0=a_k_norm[0][None, :].astype(F32),
        kn1=_tile_heads(a_k_norm[1], N_KV_A), kn2=_tile_heads(a_k_norm[2], N_KV_A),
        gb=_pad_cols(a_gate_b[None, :].astype(F32), LANES),
        posr=a_cmp_pos.reshape(2, 2, 1, half).astype(F32),
        w1r=a_cmp_w1.reshape(2, 2, half, CMP_HIDDEN).astype(BF16),
        w2=a_cmp_w2.astype(BF16))


def _row_groups(x, n_kvg):
    rows = x.shape[0]
    r = rows // CMP_STRIDE
    xr = x.reshape(r // ROWS_PER_PAGE, ROWS_PER_PAGE, CMP_STRIDE, n_kvg, HEAD_DIM)
    return jnp.transpose(xr, (0, 3, 1, 2, 4)).reshape(r // ROWS_PER_PAGE, n_kvg, ROWS_PER_PAGE,
                                                      CMP_STRIDE * HEAD_DIM)


def _split_kc_vc(cmp_out):
    nb, _, rows, _ = cmp_out.shape
    both = jnp.transpose(cmp_out.reshape(nb, 2, N_KV_A, rows, HEAD_DIM), (0, 1, 3, 2, 4))
    both = both.reshape(nb, 2, rows, N_KV_A * HEAD_DIM)
    return both[:, 0], both[:, 1]


def _nsa_prompt(xp, g_mix, wts, tm, tq_cmp, tq, tk):
    t, d = xp.shape
    bd = _seg_matrix()
    q, kv_cmp, kv_sel, kv_win, gates, sel_b, win_b = _proj_a(
        xp, g_mix, wts["wq"], wts["wkv"], wts["wg"], bd, wts["qn"], wts["kn1"], wts["kn2"], wts["gb"], tm)
    n_kvg = 2 * N_KV_A
    xr = _row_groups(kv_cmp, n_kvg)
    pt = jnp.arange(xr.shape[0], dtype=jnp.int32)[None, :]
    kc, vc = _split_kc_vc(_compress(pt, xr, wts["posr"], wts["w1r"], wts["w2"], wts["kn0"]))
    n_cmp = (t - CMP_BLOCK) // CMP_STRIDE + 1
    n_sel = -(-t // SEL_BLOCK)
    ns_pad = -(-n_sel // LANES) * LANES
    ov = _overlap_matrix(kc.shape[1], n_cmp, ns_pad, n_sel)
    pos = jnp.arange(t, dtype=jnp.int32)[:, None]
    o_cmp, sb = _cmp_topk(q, pos, kc, vc, ov, gates, tq_cmp, t // tq_cmp, n_cmp, n_sel)
    per_tile = tk // SEL_BLOCK
    sbt = sb.reshape(t, N_KV_A, ns_pad)[:, :, :(t // tk) * per_tile].reshape(t, N_KV_A, t // tk, per_tile)
    sbt = jnp.transpose(sbt, (2, 0, 1, 3)).reshape(t // tk, t, N_KV_A * per_tile)
    o = _flash("sel", q, sel_b, (sbt, o_cmp, gates), tq, tk)
    o = _flash("win", q, win_b, (o, gates), tq, tk)
    return o, kv_cmp, kv_sel, kv_win


def _nsa_sample(xs, g_mix, wts, cache_cmp, cache_sel, cache_win, page_table, tm):
    nb, n_new, d = xs.shape
    n_pages = page_table.shape[1]
    past = n_pages * PAGE
    bd = _seg_matrix()
    q, kv_cmp, kv_sel, kv_win, gates, _, _ = _proj_a(xs.reshape(nb * n_new, d), g_mix, wts["wq"], wts["wkv"],
                                                     wts["wg"], bd, wts["qn"], wts["kn1"], wts["kn2"],
                                                     wts["gb"], tm)
    n_kvg = 2 * N_KV_A
    kvw = n_kvg * HEAD_DIM
    kw = N_KV_A * HEAD_DIM
    pool = cache_cmp.shape[0]
    xr = _row_groups(cache_cmp.reshape(pool * PAGE, kvw), n_kvg)
    kc, vc = _split_kc_vc(_compress(page_table, xr, wts["posr"], wts["w1r"], wts["w2"], wts["kn0"]))
    total = past + n_new
    n_cmp = (total - CMP_BLOCK) // CMP_STRIDE + 1
    assert n_cmp <= kc.shape[1] and (n_cmp - 1) * CMP_STRIDE + CMP_BLOCK <= past
    n_sel = -(-total // SEL_BLOCK)
    ns_pad = -(-n_sel // LANES) * LANES
    ov = _overlap_matrix(kc.shape[1], n_cmp, ns_pad, n_sel)
    rows_pad = 8
    qp = _pad_rows(q.reshape(nb, n_new, d), rows_pad).reshape(nb * rows_pad, d)
    gp = _pad_rows(gates.reshape(nb, n_new, LANES), rows_pad).reshape(nb * rows_pad, LANES)
    pos = jnp.tile(past + jnp.minimum(jnp.arange(rows_pad, dtype=jnp.int32), n_new - 1), nb)[:, None]
    o_cmp, sb = _cmp_topk(qp, pos, kc, vc, ov, gp, rows_pad, 1, n_cmp, n_sel)
    n_rows = n_new * N_HEADS
    o_cmp = o_cmp.reshape(nb, rows_pad, d)[:, :n_new].reshape(nb, n_rows, HEAD_DIM)
    sb = sb.reshape(nb, rows_pad, N_KV_A, ns_pad)[:, :n_new]
    sb = jnp.repeat(sb, GROUP_A, axis=2).reshape(nb, n_rows, ns_pad)
    qbd = _block_diag_q(q.reshape(nb, n_new, N_HEADS, HEAD_DIM), N_KV_A)
    slope_rows = jnp.asarray(np.tile(_alibi_slopes(), n_new)[:, None])
    g3 = gates.reshape(nb, n_new, LANES)
    scale_sel = g3[:, :, N_HEADS:2 * N_HEADS].reshape(nb, n_rows, 1)
    scale_win = g3[:, :, 2 * N_HEADS:3 * N_HEADS].reshape(nb, n_rows, 1)
    new_sel = kv_sel.reshape(nb, n_new, kvw)
    o = _decode("sel", page_table, qbd, slope_rows, scale_sel, o_cmp,
                _pad_rows(new_sel[:, :, :kw], PAGE), _pad_rows(new_sel[:, :, kw:], PAGE), sb,
                _pages_t(cache_sel), None, n_pp=min(32, n_pages), n_new=n_new, past=past, key_base=0)
    wb = cache_win.shape[1]
    assert wb == WINDOW
    win_pages = wb // PAGE
    win_t = _pages_t(cache_win.reshape(nb * win_pages, PAGE, 2, N_KV_A, HEAD_DIM))
    win_pt = jnp.arange(nb * win_pages, dtype=jnp.int32).reshape(nb, win_pages)
    new_win = kv_win.reshape(nb, n_new, kvw)
    o = _decode("win", win_pt, qbd, slope_rows, scale_win, o,
                _pad_rows(new_win[:, :, :kw], PAGE), _pad_rows(new_win[:, :, kw:], PAGE), None,
                win_t, None, n_pp=win_pages, n_new=n_new, past=past, key_base=past - wb)
    shape5 = (nb, n_new, 2, N_KV_A, HEAD_DIM)
    win_all = jnp.concatenate([cache_win, kv_win.reshape(shape5)], axis=1)[:, n_new:]
    return (o.reshape(nb * n_new, d), kv_cmp.reshape(shape5), kv_sel.reshape(shape5), win_all)


def _fox_weights(b_w_in, b_q_norm, b_k_norm, b_f_bias):
    d = N_HEADS * HEAD_DIM
    w = b_w_in.astype(BF16)
    return dict(wq=w[:, :d], wk=w[:, d:2 * d], wv=w[:, 2 * d:3 * d], wf=_pad_cols(w[:, 3 * d:], LANES),
                qn=_tile_heads(b_q_norm, N_HEADS), kn=_tile_heads(b_k_norm, N_HEADS),
                fb=_pad_cols(b_f_bias[None, :].astype(F32), LANES))


def _fox_prompt(xp, g_mix, wts, tm, tq, tk):
    bd = _seg_matrix()
    q, kv, lf, kv_b = _proj_b(xp, g_mix, wts["wq"], wts["wk"], wts["wv"], wts["wf"], bd,
                              wts["qn"], wts["kn"], wts["fb"], tm)
    logf = lf[:, :N_HEADS]
    ct2 = _cumsum_t(logf.T, min(512, xp.shape[0]))
    o = _flash("fox", q, kv_b, (ct2,), tq, tk)
    return o, kv, logf


def _fox_sample(xs, g_mix, wts, cache_kv, cache_logf, page_table, tm):
    nb, n_new, d = xs.shape
    n_pages = page_table.shape[1]
    past = n_pages * PAGE
    bd = _seg_matrix()
    q, kv, lf, _ = _proj_b(xs.reshape(nb * n_new, d), g_mix, wts["wq"], wts["wk"], wts["wv"], wts["wf"], bd,
                           wts["qn"], wts["kn"], wts["fb"], tm)
    logf = lf[:, :N_HEADS]
    n_rows = n_new * N_HEADS
    qbd = _block_diag_q(q.reshape(nb, n_new, N_HEADS, HEAD_DIM), N_HEADS)
    lfn = jnp.transpose(logf.reshape(nb, n_new, N_HEADS), (0, 2, 1))
    lfn = jnp.tile(jnp.pad(lfn, ((0, 0), (0, 0), (0, PAGE - n_new))), (1, n_new, 1))
    ones = jnp.ones((nb, n_rows, 1), F32)
    zeros = jnp.zeros((nb, n_rows, HEAD_DIM), F32)
    new_kv = kv.reshape(nb, n_new, 2 * d)
    o = _decode("fox", page_table, qbd, jnp.zeros((n_rows, 1), F32), ones, zeros,
                _pad_rows(new_kv[:, :, :d], PAGE), _pad_rows(new_kv[:, :, d:], PAGE), lfn,
                _pages_t(cache_kv), jnp.transpose(cache_logf, (0, 2, 1)),
                n_pp=min(8, n_pages), n_new=n_new, past=past, key_base=0)
    return (o.reshape(nb * n_new, d), kv.reshape(nb, n_new, 2, N_HEADS, HEAD_DIM),
            logf.reshape(nb, n_new, N_HEADS))


def kernel(x_prompt, x_sample, cache_a_cmp_kv, cache_a_sel_kv, cache_a_win_kv, cache_b_kv, cache_b_logf,
           page_table, norm_mix, norm_ffn, a_w_in, a_q_norm, a_k_norm, a_gate_b, a_cmp_pos, a_cmp_w1,
           a_cmp_w2, a_w_out, b_w_in, b_q_norm, b_k_norm, b_f_bias, b_w_out, f_w_gate, f_w_up, f_w_down,
           m_router, m_w_gate, m_w_up, m_w_down):
    batch, seq, d = x_prompt.shape
    nb, n_new, _ = x_sample.shape
    assert batch == 1
    ts = nb * n_new
    tm_p = min(512, seq)
    tq_cmp = min(256, seq)
    tq = min(512, seq)
    tk = min(512, seq)
    tm_f = min(512, seq)
    dff = f_w_gate.shape[2]
    tf = dff // 2 if (dff // 2) % LANES == 0 else dff
    xp = x_prompt.reshape(seq, d)
    xs = x_sample.reshape(ts, d)
    gm = norm_mix.astype(F32)[:, None, :]
    gf = norm_ffn.astype(F32)[:, None, :]

    wa = _nsa_weights(a_w_in[0], a_q_norm[0], a_k_norm[0], a_gate_b[0], a_cmp_pos[0], a_cmp_w1[0],
                      a_cmp_w2[0], d)
    op, cmp_p, sel_p, win_p = _nsa_prompt(xp, gm[0], wa, tm_p, tq_cmp, tq, tk)
    os_, cmp_s, sel_s, win_s = _nsa_sample(x_sample, gm[0], wa, cache_a_cmp_kv[0], cache_a_sel_kv[0],
                                           cache_a_win_kv[0], page_table, ts)
    w_out = a_w_out[0].astype(BF16)
    fg, fu, fd = f_w_gate[0].astype(BF16), f_w_up[0].astype(BF16), f_w_down[0].astype(BF16)
    xp = _ffn(xp, op, w_out, gf[0], fg, fu, fd, tm_f, tf)
    xs = _ffn(xs, os_, w_out, gf[0], fg, fu, fd, ts, tf)

    wb = _fox_weights(b_w_in[0], b_q_norm[0], b_k_norm[0], b_f_bias[0])
    op, kv_p, lf_p = _fox_prompt(xp, gm[1], wb, tm_p, tq, tk)
    os_, kv_s, lf_s = _fox_sample(xs.reshape(nb, n_new, d), gm[1], wb, cache_b_kv[0], cache_b_logf[0],
                                  page_table, ts)
    w_out = b_w_out[0].astype(BF16)
    router = _pad_cols(m_router[0].astype(BF16), LANES)
    mg, mu, md = m_w_gate[0].astype(BF16), m_w_up[0].astype(BF16), m_w_down[0].astype(BF16)
    xp = _moe(xp, op, w_out, gf[1], router, mg, mu, md, tm_f, tf)
    xs = _moe(xs, os_, w_out, gf[1], router, mg, mu, md, ts, tf)

    wlen = min(WINDOW, seq)
    kv5 = lambda a, rows: a.reshape(1, 1, rows, 2, N_KV_A, HEAD_DIM)
    return (xp.reshape(batch, seq, d), xs.reshape(nb, n_new, d),
            kv5(cmp_p, seq), cmp_s[None],
            kv5(sel_p, seq), sel_s[None],
            kv5(win_p[seq - wlen:], wlen), win_s[None],
            kv_p.reshape(1, 1, seq, 2, N_HEADS, HEAD_DIM), kv_s[None],
            lf_p.reshape(1, 1, seq, N_HEADS), lf_s[None])
```
